```python
import math
import jax
import jax.numpy as jnp
from jax import lax
import numpy as np

D_MODEL = 4096
BATCH = 32
SEQ = 256
DEPTH = 2
DEC_BATCH = 2
DEC_SEQ = 2048
PAST_LEN = 512

HEAD_DIM = 128
GDN_HEADS = 8
NA_HEADS = 16
FN_GROUPS = 8
GDN_WIDTH = GDN_HEADS * HEAD_DIM
NA_WIDTH = NA_HEADS * HEAD_DIM
FN_WIDTH = FN_GROUPS * HEAD_DIM
MIX_WIDTH = GDN_WIDTH + NA_WIDTH + FN_WIDTH
FFN_HIDDEN = -(-8 * D_MODEL // (3 * 256)) * 256
CONV_K = 5
CHUNK = 64
GRID_W = 64
NA_ROWS = 8
NA_COLS = 16
Q_BLOCK = 128
ROPE_BASE = 10000.0
EPS = 1e-6

OFF_GDN_Z = 3 * GDN_WIDTH
OFF_GDN_A = 4 * GDN_WIDTH
OFF_GDN_B = OFF_GDN_A + 2 * GDN_HEADS
OFF_NA = OFF_GDN_B + 2 * GDN_HEADS
OFF_FN = OFF_NA + 3 * NA_WIDTH
IN_WIDTH = OFF_FN + FN_WIDTH

kernel_name = "hybrid_gdn_na_fnet_diffusion_step"

F32 = jnp.float32


def rms_norm(x, w):
    xf = x.astype(F32)
    y = xf * lax.rsqrt(jnp.mean(xf * xf, axis=-1, keepdims=True) + EPS)
    return (y * w.astype(F32)).astype(x.dtype)


def l2_norm(x):
    xf = x.astype(F32)
    return (xf * lax.rsqrt(jnp.sum(xf * xf, axis=-1, keepdims=True) + EPS)).astype(x.dtype)


def adaln_params(cond, ada_w, ada_b):
    m = jnp.dot(jax.nn.silu(cond), ada_w) + ada_b
    return jnp.split(m[:, None, :], 6, axis=-1)


def modulate(x, gain, shift, scale):
    return rms_norm(x, gain) * (1 + scale) + shift


def _rotate_half(x, ang):
    cos = jnp.cos(ang)[None, :, None, :]
    sin = jnp.sin(ang)[None, :, None, :]
    x1, x2 = jnp.split(x, 2, axis=-1)
    return jnp.concatenate([x1 * cos - x2 * sin, x2 * cos + x1 * sin], axis=-1)


def axial_rope(x):
    T, dim = x.shape[1], x.shape[-1]
    half = dim // 2
    t = jnp.arange(T)
    inv_freq = 1.0 / (ROPE_BASE ** (jnp.arange(0, half, 2, dtype=F32) / half))
    ang_row = (t // GRID_W).astype(F32)[:, None] * inv_freq
    ang_col = (t % GRID_W).astype(F32)[:, None] * inv_freq
    xf = x.astype(F32)
    y = jnp.concatenate([_rotate_half(xf[..., :half], ang_row), _rotate_half(xf[..., half:], ang_col)], axis=-1)
    return y.astype(x.dtype)


def centred_depthwise_conv(x, w):
    k, ch = w.shape
    return lax.conv_general_dilated(x, w[:, None, :].astype(x.dtype), window_strides=(1,),
                                    padding=[(k // 2, k // 2)],
                                    dimension_numbers=("NWC", "WIO", "NWC"),
                                    feature_group_count=ch)


def split_projection(h, in_w):
    proj = jnp.dot(h, in_w)
    return (proj[..., :OFF_GDN_Z], proj[..., OFF_GDN_Z:OFF_GDN_A], proj[..., OFF_GDN_A:OFF_GDN_B],
            proj[..., OFF_GDN_B:OFF_NA], proj[..., OFF_NA:OFF_FN], proj[..., OFF_FN:])


def gdn_qkv(qkv, conv_w, rotary):
    B, T, _ = qkv.shape
    u = jax.nn.silu(centred_depthwise_conv(qkv, conv_w))
    q, k, v = (t.reshape(B, T, GDN_HEADS, HEAD_DIM) for t in jnp.split(u, 3, axis=-1))
    q, k = l2_norm(q), l2_norm(k)
    if rotary:
        q, k = axial_rope(q), axial_rope(k)
    return q, k, v


def gdn_gates(a, b, a_log, dt_bias):
    B, T, _ = a.shape
    a = a.reshape(B, T, 2, GDN_HEADS).astype(F32)
    b = b.reshape(B, T, 2, GDN_HEADS).astype(F32)
    g = -jnp.exp(a_log.astype(F32)) * jax.nn.softplus(a + dt_bias.astype(F32))
    return g, jax.nn.sigmoid(b)


def gated_delta_chunked(q, k, v, g, beta, s0):
    B, T, H, DK = q.shape
    DV = v.shape[-1]
    N = T // CHUNK
    out_dtype = v.dtype

    def chunks(t):
        t = t.astype(F32).reshape((B, N, CHUNK) + t.shape[2:])
        return jnp.moveaxis(t, 3, 1)

    q, k, v, g, beta = (chunks(t) for t in (q, k, v, g, beta))
    q = q * (DK ** -0.5)
    g_cum = jnp.cumsum(g, axis=-1)
    idx = jnp.arange(CHUNK)
    causal = idx[:, None] >= idx[None, :]
    strict = idx[:, None] > idx[None, :]
    decay = jnp.exp(jnp.where(causal, g_cum[..., :, None] - g_cum[..., None, :], -jnp.inf))
    k_beta = k * beta[..., None]
    lower = jnp.where(strict, jnp.einsum('bhncd,bhnsd->bhncs', k_beta, k) * decay, 0.0)
    eye = jnp.eye(CHUNK, dtype=F32)
    rhs = jnp.concatenate([v * beta[..., None], k_beta * jnp.exp(g_cum)[..., None]], axis=-1)
    sol = lax.linalg.triangular_solve(eye + lower, rhs, left_side=True, lower=True, unit_diagonal=True)
    u, w = sol[..., :DV], sol[..., DV:]
    intra = jnp.einsum('bhncd,bhnsd->bhncs', q, k) * decay
    q_dec = q * jnp.exp(g_cum)[..., None]
    k_dec = k * jnp.exp(g_cum[..., -1:] - g_cum)[..., None]
    chunk_decay = jnp.exp(g_cum[..., -1])

    def step(S, xs):
        q_c, k_c, u_c, w_c, a_c, d_c = xs
        v_new = u_c - jnp.einsum('bhcd,bhde->bhce', w_c, S)
        o = jnp.einsum('bhcd,bhde->bhce', q_c, S) + jnp.einsum('bhcs,bhse->bhce', a_c, v_new)
        S = S * d_c[..., None, None] + jnp.einsum('bhcd,bhce->bhde', k_c, v_new)
        return S, o

    xs = tuple(jnp.moveaxis(t, 2, 0) for t in (q_dec, k_dec, u, w, intra, chunk_decay))
    s_final, o = lax.scan(step, s0.astype(F32), xs)
    o = o.transpose(1, 0, 3, 2, 4).reshape(B, T, H, DV)
    return o.astype(out_dtype), s_final.astype(out_dtype)


def bidirectional_gdn(q, k, v, g, beta, s0):
    o_f, s_f = gated_delta_chunked(q, k, v, g[:, :, 0], beta[:, :, 0], s0[:, 0])
    rev = lambda t: jnp.flip(t, axis=1)
    o_b, s_b = gated_delta_chunked(rev(q), rev(k), rev(v), rev(g[:, :, 1]), rev(beta[:, :, 1]), s0[:, 1])
    return o_f + rev(o_b), jnp.stack([s_f, s_b], axis=1)


def gdn_output(o, z, norm_w):
    B, T = o.shape[:2]
    y = rms_norm(o, norm_w) * jax.nn.silu(z.reshape(B, T, GDN_HEADS, HEAD_DIM))
    return y.reshape(B, T, GDN_WIDTH)


def na_heads(na_in, q_norm_w, k_norm_w):
    B, T, _ = na_in.shape
    q, k, v = (t.reshape(B, T, NA_HEADS, HEAD_DIM) for t in jnp.split(na_in, 3, axis=-1))
    return rms_norm(q, q_norm_w), rms_norm(k, k_norm_w), v


def context_attention(q, k, v):
    B, T, H, D = q.shape
    scale = D ** -0.5
    qb = q.reshape(B, T // Q_BLOCK, Q_BLOCK, H, D).transpose(1, 0, 2, 3, 4)

    def block(qi):
        s = jnp.einsum('bqhd,bkhd->bhqk', qi, k).astype(F32) * scale
        p = jax.nn.softmax(s, axis=-1).astype(v.dtype)
        return jnp.einsum('bhqk,bkhd->bqhd', p, v)

    o = lax.map(block, qb)
    return o.transpose(1, 0, 2, 3, 4).reshape(B, T, H * D)


def neighbourhood_attention(q, k, v, ctx_k, ctx_v, rpb):
    B, T, H, D = q.shape
    rows = T // GRID_W
    wr = min(NA_ROWS, rows)
    r = jnp.arange(rows)
    row_idx = jnp.clip(r - wr // 2, 0, rows - wr)[:, None] + jnp.arange(wr)[None, :]
    col = jnp.arange(GRID_W)
    col_start = jnp.clip(col - NA_COLS // 2, 0, GRID_W - NA_COLS)
    in_win = (col[None, :] >= col_start[:, None]) & (col[None, :] < col_start[:, None] + NA_COLS)
    dr = row_idx - r[:, None] + (NA_ROWS - 1)
    dc = jnp.clip(col[None, :] - col[:, None], -(NA_COLS - 1), NA_COLS - 1) + (NA_COLS - 1)
    bias = rpb[:, dr[:, None, :, None], dc[None, :, None, :]].astype(F32)
    bias = jnp.where(in_win[None, None, :, None, :], bias, -jnp.inf)
    scale = D ** -0.5
    qg = q.reshape(B, rows, GRID_W, H, D)
    kg = k.reshape(B, rows, GRID_W, H, D)[:, row_idx]
    vg = v.reshape(B, rows, GRID_W, H, D)[:, row_idx]
    s_loc = jnp.einsum('brqhd,brjkhd->bhrqjk', qg, kg).astype(F32) * scale + bias[None]
    s_ctx = jnp.einsum('brqhd,bmhd->bhrqm', qg, ctx_k).astype(F32) * scale
    n_loc = wr * GRID_W
    s = jnp.concatenate([s_loc.reshape(B, H, rows, GRID_W, n_loc), s_ctx], axis=-1)
    p = jax.nn.softmax(s, axis=-1).astype(v.dtype)
    p_loc = p[..., :n_loc].reshape(B, H, rows, GRID_W, wr, GRID_W)
    o = (jnp.einsum('bhrqjk,brjkhd->brqhd', p_loc, vg)
         + jnp.einsum('bhrqm,bmhd->brqhd', p[..., n_loc:], ctx_v))
    return o.reshape(B, T, H * D)


def fourier_mix(h):
    B, T, _ = h.shape
    hg = h.reshape(B, T, FN_GROUPS, HEAD_DIM).astype(F32)
    y = jnp.fft.fft2(hg, axes=(1, 3), norm="ortho").real
    return y.reshape(B, T, FN_WIDTH).astype(h.dtype)


def residual_tail(x, mixed, mods, p):
    _, _, gate1, shift2, scale2, gate2 = mods
    x = x + gate1 * jnp.dot(mixed, p["out_w"])
    h = modulate(x, p["norm2"], shift2, scale2)
    ffn = jnp.dot(jax.nn.silu(jnp.dot(h, p["w_gate"])) * jnp.dot(h, p["w_up"]), p["w_down"])
    return x + gate2 * ffn


def context_layer(x, mods, p):
    B = x.shape[0]
    shift1, scale1 = mods[0], mods[1]
    h = modulate(x, p["norm1"], shift1, scale1)
    qkv_g, z, a, b, na_in, fn_in = split_projection(h, p["in_w"])
    q, k, v = gdn_qkv(qkv_g, p["conv_w"], rotary=False)
    g, beta = gdn_gates(a, b, p["a_log"], p["dt_bias"])
    s0 = jnp.zeros((B, 2, GDN_HEADS, HEAD_DIM, HEAD_DIM), F32)
    o, s_ctx = bidirectional_gdn(q, k, v, g, beta, s0)
    gdn_out = gdn_output(o, z, p["gdn_norm"])
    nq, nk, nv = na_heads(na_in, p["q_norm"], p["k_norm"])
    na_out = context_attention(nq, nk, nv)
    fn_out = fourier_mix(fn_in)
    mixed = jnp.concatenate([gdn_out, na_out, fn_out], axis=-1)
    return residual_tail(x, mixed, mods, p), nk, nv, s_ctx


def latent_layer(x, mods, p, ctx_k, ctx_v, ctx_state):
    shift1, scale1 = mods[0], mods[1]
    h = modulate(x, p["norm1"], shift1, scale1)
    qkv_g, z, a, b, na_in, fn_in = split_projection(h, p["in_w"])
    q, k, v = gdn_qkv(qkv_g, p["conv_w"], rotary=True)
    g, beta = gdn_gates(a, b, p["a_log"], p["dt_bias"])
    o, _ = bidirectional_gdn(q, k, v, g, beta, ctx_state)
    gdn_out = gdn_output(o, z, p["gdn_norm"])
    nq, nk, nv = na_heads(na_in, p["q_norm"], p["k_norm"])
    na_out = neighbourhood_attention(nq, nk, nv, ctx_k, ctx_v, p["rpb"])
    fn_out = fourier_mix(fn_in)
    mixed = jnp.concatenate([gdn_out, na_out, fn_out], axis=-1)
    return residual_tail(x, mixed, mods, p)


def setup_inputs(seed: int = 0) -> dict:
    key = jax.random.key(seed)
    ks = jax.random.split(key, 24)
    nrm = lambda k, shape, s: jax.random.normal(k, shape, F32) * s
    x_prompt = nrm(ks[0], (BATCH, SEQ, D_MODEL), 1.0)
    x_sample = nrm(ks[1], (DEC_BATCH, DEC_SEQ, D_MODEL), 1.0)
    cache_na_k = nrm(ks[2], (DEC_BATCH, DEPTH, PAST_LEN, NA_HEADS, HEAD_DIM), 1.0)
    cache_na_v = nrm(ks[3], (DEC_BATCH, DEPTH, PAST_LEN, NA_HEADS, HEAD_DIM), 1.0)
    state_gdn = nrm(ks[4], (DEC_BATCH, DEPTH, 2, GDN_HEADS, HEAD_DIM, HEAD_DIM), HEAD_DIM ** -0.5)
    c = nrm(ks[5], (DEC_BATCH, D_MODEL), 1.0)
    c_ctx = nrm(ks[6], (D_MODEL,), 1.0)
    norm1_w = 1.0 + nrm(ks[7], (DEPTH, D_MODEL), 0.05)
    norm2_w = 1.0 + nrm(ks[8], (DEPTH, D_MODEL), 0.05)
    ada_w = nrm(ks[9], (DEPTH, D_MODEL, 6 * D_MODEL), 0.5 * D_MODEL ** -0.5)
    ada_b = nrm(ks[10], (DEPTH, 6 * D_MODEL), 0.02)
    in_w = nrm(ks[11], (DEPTH, D_MODEL, IN_WIDTH), D_MODEL ** -0.5)
    conv_w = nrm(ks[12], (DEPTH, CONV_K, 3 * GDN_WIDTH), CONV_K ** -0.5)
    gdn_a_log = jnp.log(jax.random.uniform(ks[13], (DEPTH, 2, GDN_HEADS), F32, 1.0, 16.0))
    dt = jnp.exp(jax.random.uniform(ks[14], (DEPTH, 2, GDN_HEADS), F32, math.log(1e-3), math.log(1e-1)))
    gdn_dt_bias = dt + jnp.log(-jnp.expm1(-dt))
    gdn_norm_w = 1.0 + nrm(ks[15], (DEPTH, HEAD_DIM), 0.05)
    na_q_norm_w = 1.0 + nrm(ks[16], (DEPTH, HEAD_DIM), 0.05)
    na_k_norm_w = 1.0 + nrm(ks[17], (DEPTH, HEAD_DIM), 0.05)
    na_rpb = nrm(ks[18], (DEPTH, NA_HEADS, 2 * NA_ROWS - 1, 2 * NA_COLS - 1), 0.1)
    out_w = nrm(ks[19], (DEPTH, MIX_WIDTH, D_MODEL), MIX_WIDTH ** -0.5)
    ffn_gate_w = nrm(ks[20], (DEPTH, D_MODEL, FFN_HIDDEN), D_MODEL ** -0.5)
    ffn_up_w = nrm(ks[21], (DEPTH, D_MODEL, FFN_HIDDEN), D_MODEL ** -0.5)
    ffn_down_w = nrm(ks[22], (DEPTH, FFN_HIDDEN, D_MODEL), FFN_HIDDEN ** -0.5)
    return {"x_prompt": x_prompt, "x_sample": x_sample, "cache_na_k": cache_na_k,
            "cache_na_v": cache_na_v, "state_gdn": state_gdn, "c": c, "c_ctx": c_ctx,
            "norm1_w": norm1_w, "norm2_w": norm2_w, "ada_w": ada_w, "ada_b": ada_b,
            "in_w": in_w, "conv_w": conv_w, "gdn_a_log": gdn_a_log, "gdn_dt_bias": gdn_dt_bias,
            "gdn_norm_w": gdn_norm_w, "na_q_norm_w": na_q_norm_w, "na_k_norm_w": na_k_norm_w,
            "na_rpb": na_rpb, "out_w": out_w, "ffn_gate_w": ffn_gate_w, "ffn_up_w": ffn_up_w,
            "ffn_down_w": ffn_down_w}


def reference(x_prompt, x_sample, cache_na_k, cache_na_v, state_gdn, c, c_ctx,
              norm1_w, norm2_w, ada_w, ada_b, in_w, conv_w, gdn_a_log, gdn_dt_bias,
              gdn_norm_w, na_q_norm_w, na_k_norm_w, na_rpb, out_w, ffn_gate_w, ffn_up_w, ffn_down_w):
    y_prompt = x_prompt
    y_sample = x_sample
    new_k, new_v, new_s = [], [], []
    for l in range(DEPTH):
        p = {"norm1": norm1_w[l], "norm2": norm2_w[l], "in_w": in_w[l], "conv_w": conv_w[l],
             "a_log": gdn_a_log[l], "dt_bias": gdn_dt_bias[l], "gdn_norm": gdn_norm_w[l],
             "q_norm": na_q_norm_w[l], "k_norm": na_k_norm_w[l], "rpb": na_rpb[l],
             "out_w": out_w[l], "w_gate": ffn_gate_w[l], "w_up": ffn_up_w[l], "w_down": ffn_down_w[l]}
        ctx_mods = adaln_params(c_ctx[None, :], ada_w[l], ada_b[l])
        lat_mods = adaln_params(c, ada_w[l], ada_b[l])
        y_prompt, k_l, v_l, s_l = context_layer(y_prompt, ctx_mods, p)
        new_k.append(k_l)
        new_v.append(v_l)
        new_s.append(s_l)
        y_sample = latent_layer(y_sample, lat_mods, p, cache_na_k[:, l], cache_na_v[:, l], state_gdn[:, l])
    new_na_k = jnp.stack(new_k, axis=1)
    new_na_v = jnp.stack(new_v, axis=1)
    new_state_gdn = jnp.stack(new_s, axis=1)
    return (y_prompt, y_sample, new_na_k, new_na_v, new_state_gdn)
```

```python
import functools
import math

import jax
import jax.numpy as jnp
from jax import lax
from jax.experimental import pallas as pl
from jax.experimental.pallas import tpu as pltpu

F32 = jnp.float32
BF16 = jnp.bfloat16

HEAD_DIM = 128
CONV_K = 5
CHUNK = 64
GROUP = 256
GRID_W = 64
NA_ROWS = 8
NA_COLS = 16
ROPE_BASE = 10000.0
EPS = 1e-6
MOD_ROWS = 8
CONV_PAD = 8
V7X_VMEM_BYTES = 64 * 1024 * 1024
VMEM_CAP = V7X_VMEM_BYTES - 8 * 1024 * 1024


def _cparams(sem, vmem_bytes):
    limit = int(min(max(vmem_bytes, 16 * 1024 * 1024), VMEM_CAP))
    return pltpu.CompilerParams(dimension_semantics=sem, vmem_limit_bytes=limit)


def _tile(n, pref, align):
    if n <= pref:
        return n
    t = (pref // align) * align
    while t >= align:
        if n % t == 0:
            return t
        t -= align
    return n


def _sigmoid(x):
    return 1.0 / (1.0 + jnp.exp(-x))


def _silu(x):
    return x * _sigmoid(x)


def _softplus(x):
    return jnp.maximum(x, 0.0) + jnp.log(1.0 + jnp.exp(-jnp.abs(x)))


def _dot(a, b):
    return jnp.dot(a, b, preferred_element_type=F32)


def _dot_nt(a, b):
    return lax.dot_general(a, b, (((1,), (1,)), ((), ())), preferred_element_type=F32)


def _split2(x):
    hi = x.astype(BF16)
    lo = (x - hi.astype(F32)).astype(BF16)
    return hi, lo


def _dot3(a, b):
    a_hi, a_lo = a
    b_hi, b_lo = b
    return _dot(a_hi, b_hi) + _dot(a_hi, b_lo) + _dot(a_lo, b_hi)


def _adaln_kernel(c_ref, w_ref, b_ref, o_ref):
    s = _silu(c_ref[...]).astype(BF16)
    o_ref[...] = _dot(s, w_ref[...].astype(BF16)) + b_ref[...]


def _adaln(cond, ada_w, ada_b):
    depth, d, n = ada_w.shape
    tn = _tile(n, 512, 128)
    return pl.pallas_call(
        _adaln_kernel,
        grid=(depth, n // tn),
        in_specs=[pl.BlockSpec((MOD_ROWS, d), lambda l, j: (0, 0)),
                  pl.BlockSpec((None, d, tn), lambda l, j: (l, 0, j)),
                  pl.BlockSpec((None, 1, tn), lambda l, j: (l, 0, j))],
        out_specs=pl.BlockSpec((None, MOD_ROWS, tn), lambda l, j: (l, 0, j)),
        out_shape=jax.ShapeDtypeStruct((depth, MOD_ROWS, n), F32),
        compiler_params=_cparams(("parallel", "arbitrary"), 3 * d * tn * 4 + (4 << 20)),
        name="adaln",
    )(cond, ada_w, ada_b.reshape(depth, 1, n))


class _Rows:
    def __init__(self, n_ctx, lat_seq, tm):
        assert n_ctx % tm == 0 and lat_seq % tm == 0
        self.ctx_tiles = n_ctx // tm
        self.seq_tiles = lat_seq // tm

    def mod_index(self, i, which):
        row = jnp.where(i < self.ctx_tiles, 0, 1 + (i - self.ctx_tiles) // self.seq_tiles)
        return row * 6 + which


def _modulate_kernel(x_ref, g_ref, sh_ref, sc_ref, o_ref):
    x = x_ref[...]
    y = x * lax.rsqrt(jnp.mean(x * x, axis=-1, keepdims=True) + EPS) * g_ref[...]
    o_ref[...] = (y * (1.0 + sc_ref[...]) + sh_ref[...]).astype(o_ref.dtype)


def _modulate(x, gain, mods, shift_id, scale_id, n_ctx, lat_seq):
    nt, d = x.shape
    tm = _tile(math.gcd(n_ctx, lat_seq), 256, 8)
    rows = _Rows(n_ctx, lat_seq, tm)
    return pl.pallas_call(
        _modulate_kernel,
        grid=(nt // tm,),
        in_specs=[pl.BlockSpec((tm, d), lambda i: (i, 0)),
                  pl.BlockSpec((1, d), lambda i: (0, 0)),
                  pl.BlockSpec((None, 1, d), lambda i: (rows.mod_index(i, shift_id), 0, 0)),
                  pl.BlockSpec((None, 1, d), lambda i: (rows.mod_index(i, scale_id), 0, 0))],
        out_specs=pl.BlockSpec((tm, d), lambda i: (i, 0)),
        out_shape=jax.ShapeDtypeStruct((nt, d), BF16),
        compiler_params=_cparams(("parallel",), 16 * tm * d + (4 << 20)),
        name="modulate",
    )(x, gain.reshape(1, d), mods, mods)


def _mm_kernel(a_ref, w_ref, o_ref):
    o_ref[...] = _dot(a_ref[...], w_ref[...]).astype(o_ref.dtype)


def _matmul(a, w, out_dtype, name):
    m, k = a.shape
    n = w.shape[1]
    tm = _tile(m, 1024, 8)
    tn = _tile(n, 512, 128)
    return pl.pallas_call(
        _mm_kernel,
        grid=(m // tm, n // tn),
        in_specs=[pl.BlockSpec((tm, k), lambda i, j: (i, 0)),
                  pl.BlockSpec((k, tn), lambda i, j: (0, j))],
        out_specs=pl.BlockSpec((tm, tn), lambda i, j: (i, j)),
        out_shape=jax.ShapeDtypeStruct((m, n), out_dtype),
        compiler_params=_cparams(("parallel", "arbitrary"),
                                 4 * tm * k + 4 * k * tn + 12 * tm * tn + (4 << 20)),
        name=name,
    )(a, w)


def _outproj_kernel(a1_ref, a2_ref, a3_ref, w1_ref, w2_ref, w3_ref, x_ref, g_ref, o_ref):
    acc = _dot(a1_ref[...], w1_ref[...]) + _dot(a2_ref[...], w2_ref[...]) + _dot(a3_ref[...], w3_ref[...])
    o_ref[...] = x_ref[...] + g_ref[...] * acc


def _outproj(parts, weights, x, mods, gate_id, n_ctx, lat_seq):
    nt, d = x.shape
    tm = _tile(math.gcd(n_ctx, lat_seq), 1024, 8)
    tn = _tile(d, 512, 128)
    rows = _Rows(n_ctx, lat_seq, tm)
    ktot = sum(p.shape[1] for p in parts)
    a_specs = [pl.BlockSpec((tm, p.shape[1]), lambda i, j: (i, 0)) for p in parts]
    w_specs = [pl.BlockSpec((w.shape[0], tn), lambda i, j: (0, j)) for w in weights]
    return pl.pallas_call(
        _outproj_kernel,
        grid=(nt // tm, d // tn),
        in_specs=a_specs + w_specs + [
            pl.BlockSpec((tm, tn), lambda i, j: (i, j)),
            pl.BlockSpec((None, 1, tn), lambda i, j: (rows.mod_index(i, gate_id), 0, j))],
        out_specs=pl.BlockSpec((tm, tn), lambda i, j: (i, j)),
        out_shape=jax.ShapeDtypeStruct((nt, d), F32),
        compiler_params=_cparams(("parallel", "arbitrary"),
                                 4 * tm * ktot + 4 * ktot * tn + 24 * tm * tn + (4 << 20)),
        name="outproj",
    )(*parts, *weights, x, mods)


def _swiglu_kernel(h_ref, wg_ref, wu_ref, o_ref):
    h = h_ref[...]
    g = _dot(h, wg_ref[...])
    u = _dot(h, wu_ref[...])
    o_ref[...] = (_silu(g) * u).astype(o_ref.dtype)


def _swiglu(h, wg, wu):
    m, k = h.shape
    n = wg.shape[1]
    tm = _tile(m, 1024, 8)
    tn = _tile(n, 512, 128)
    return pl.pallas_call(
        _swiglu_kernel,
        grid=(m // tm, n // tn),
        in_specs=[pl.BlockSpec((tm, k), lambda i, j: (i, 0)),
                  pl.BlockSpec((k, tn), lambda i, j: (0, j)),
                  pl.BlockSpec((k, tn), lambda i, j: (0, j))],
        out_specs=pl.BlockSpec((tm, tn), lambda i, j: (i, j)),
        out_shape=jax.ShapeDtypeStruct((m, n), BF16),
        compiler_params=_cparams(("parallel", "arbitrary"),
                                 4 * tm * k + 8 * k * tn + 16 * tm * tn + (4 << 20)),
        name="swiglu",
    )(h, wg, wu)


def _down_kernel(a_ref, w_ref, x_ref, g_ref, o_ref, acc_ref, *, k_steps):
    kk = pl.program_id(2)
    part = _dot(a_ref[...], w_ref[...])

    @pl.when(kk == 0)
    def _():
        acc_ref[...] = part

    @pl.when(kk > 0)
    def _():
        acc_ref[...] += part

    @pl.when(kk == k_steps - 1)
    def _():
        o_ref[...] = x_ref[...] + g_ref[...] * acc_ref[...]


def _down(a, w, x, mods, gate_id, n_ctx, lat_seq):
    nt, k = a.shape
    d = w.shape[1]
    tm = _tile(math.gcd(n_ctx, lat_seq), 1024, 8)
    tn = _tile(d, 512, 128)
    tk = _tile(k, 6144, 128)
    rows = _Rows(n_ctx, lat_seq, tm)
    return pl.pallas_call(
        functools.partial(_down_kernel, k_steps=k // tk),
        grid=(nt // tm, d // tn, k // tk),
        in_specs=[pl.BlockSpec((tm, tk), lambda i, j, s: (i, s)),
                  pl.BlockSpec((tk, tn), lambda i, j, s: (s, j)),
                  pl.BlockSpec((tm, tn), lambda i, j, s: (i, j)),
                  pl.BlockSpec((None, 1, tn), lambda i, j, s: (rows.mod_index(i, gate_id), 0, j))],
        out_specs=pl.BlockSpec((tm, tn), lambda i, j, s: (i, j)),
        out_shape=jax.ShapeDtypeStruct((nt, d), F32),
        scratch_shapes=[pltpu.VMEM((tm, tn), F32)],
        compiler_params=_cparams(("parallel", "arbitrary", "arbitrary"),
                                 4 * tm * tk + 4 * tk * tn + 28 * tm * tn + (4 << 20)),
        name="down",
    )(a, w, x, mods)


def _group_masks():
    i = lax.broadcasted_iota(jnp.int32, (GROUP, GROUP), 0)
    j = lax.broadcasted_iota(jnp.int32, (GROUP, GROUP), 1)
    same = (i // CHUNK) == (j // CHUNK)
    return same, i, j


def _neumann_solve(strict, rhs):
    p = -strict
    x = rhs
    n_fac = int(math.log2(CHUNK))
    for f in range(n_fac):
        ps = _split2(p)
        x = x + _dot3(ps, _split2(x))
        if f < n_fac - 1:
            p = _dot3(ps, ps)
    return x


def _gdn_kernel(*refs, seq_len, rotary, has_s0, emit_state):
    it = iter(refs)
    alog_ref, dtb_ref = next(it), next(it)
    q_ref, k_ref, v_ref, z_ref = next(it), next(it), next(it), next(it)
    cw_refs = (next(it), next(it), next(it))
    acol_ref, arow_ref, gw_ref = next(it), next(it), next(it)
    cos_ref = sin_ref = s0_ref = sf_ref = None
    if rotary:
        cos_ref, sin_ref = next(it), next(it)
    if has_s0:
        s0_ref = next(it)
    o_ref = next(it)
    if emit_state:
        sf_ref = next(it)
    pad_refs = (next(it), next(it), next(it))
    qs_ref, kn_ref, vc_ref, oacc_ref, s_ref = next(it), next(it), next(it), next(it), next(it)

    head = pl.program_id(1)
    n_groups = seq_len // GROUP
    n_chunks = GROUP // CHUNK
    scale = HEAD_DIM ** -0.5

    for src, pad in zip((q_ref, k_ref, v_ref), pad_refs):
        pad[pl.ds(0, CONV_PAD), :] = jnp.zeros((CONV_PAD, HEAD_DIM), F32)
        pad[pl.ds(CONV_PAD + seq_len, CONV_PAD), :] = jnp.zeros((CONV_PAD, HEAD_DIM), F32)
        pad[pl.ds(CONV_PAD, seq_len), :] = src[...]

    lane = lax.broadcasted_iota(jnp.int32, (GROUP, HEAD_DIM), 1)
    first_half = (lane % 64) < 32

    def conv_silu(pad, cw_ref, g0):
        win = pad[pl.ds(g0, GROUP + 2 * CONV_PAD), :]
        cw = cw_ref[...]
        acc = None
        for tap in range(CONV_K):
            off = CONV_PAD - CONV_K // 2 + tap
            term = win[off:off + GROUP, :] * cw[tap:tap + 1, :]
            acc = term if acc is None else acc + term
        return _silu(acc)

    def l2n(x):
        return x * lax.rsqrt(jnp.sum(x * x, axis=-1, keepdims=True) + EPS)

    def rope(x, g0):
        cos = cos_ref[pl.ds(g0, GROUP), :]
        sin = sin_ref[pl.ds(g0, GROUP), :]
        swapped = jnp.where(first_half, pltpu.roll(x, HEAD_DIM - 32, axis=1), pltpu.roll(x, 32, axis=1))
        return x * cos + swapped * sin

    def prep(gi, carry):
        g0 = pl.multiple_of(gi * GROUP, GROUP)
        q = l2n(conv_silu(pad_refs[0], cw_refs[0], g0))
        k = l2n(conv_silu(pad_refs[1], cw_refs[1], g0))
        v = conv_silu(pad_refs[2], cw_refs[2], g0)
        if rotary:
            q, k = rope(q, g0), rope(k, g0)
        qs_ref[pl.ds(g0, GROUP), :] = q * scale
        kn_ref[pl.ds(g0, GROUP), :] = k
        vc_ref[pl.ds(g0, GROUP), :] = v
        return carry

    lax.fori_loop(0, n_groups, prep, 0)

    same, ri, ci = _group_masks()

    def scan_group(gi, direction):
        fwd = direction == 0
        g0 = pl.multiple_of(gi * GROUP, GROUP)
        qs = qs_ref[pl.ds(g0, GROUP), :]
        k = kn_ref[pl.ds(g0, GROUP), :]
        v = vc_ref[pl.ds(g0, GROUP), :]
        ac = acol_ref[gi]
        ar = arow_ref[gi]
        neg_ea_c = -jnp.exp(jnp.full((GROUP, 1), alog_ref[direction, head], F32))
        neg_ea_r = -jnp.exp(jnp.full((1, GROUP), alog_ref[direction, head], F32))
        dtb = dtb_ref[direction, head]
        g_c = neg_ea_c * _softplus(ac[:, direction:direction + 1] + dtb)
        g_r = neg_ea_r * _softplus(ar[direction:direction + 1, :] + dtb)
        beta = _sigmoid(ac[:, 2 + direction:3 + direction])

        incl = same & ((ci <= ri) if fwd else (ci >= ri))
        strict = same & ((ci < ri) if fwd else (ci > ri))
        ones_incl = jnp.where(incl, 1.0, 0.0).astype(BF16)
        gcb = jnp.broadcast_to(g_c, (GROUP, HEAD_DIM))
        grb = jnp.broadcast_to(g_r, (8, GROUP))
        cum_c = jnp.zeros((GROUP, HEAD_DIM), F32)
        cum_r = jnp.zeros((8, GROUP), F32)
        for _ in range(3):
            piece_c = gcb.astype(BF16)
            piece_r = grb.astype(BF16)
            cum_c = cum_c + _dot(ones_incl, piece_c)
            cum_r = cum_r + _dot_nt(piece_r, ones_incl)
            gcb = gcb - piece_c.astype(F32)
            grb = grb - piece_r.astype(F32)
        gc = cum_c[:, 0:1]
        gr = cum_r[0:1, :]

        decay = jnp.exp(jnp.where(incl, gc - gr, -jnp.inf))
        kb = k * beta
        both = _dot_nt(jnp.concatenate([kb, qs], axis=0).astype(BF16), k.astype(BF16))
        tri = jnp.where(strict, both[:GROUP] * decay, 0.0)
        intra = (both[GROUP:] * decay).astype(BF16)
        eg = jnp.exp(gc)
        sol = _neumann_solve(tri, jnp.concatenate([v * beta, kb * eg], axis=1))
        u = sol[:, :HEAD_DIM]
        w = sol[:, HEAD_DIM:]
        q_dec = qs * eg
        k_dec_parts = []
        totals = []
        for c in range(n_chunks):
            edge = c * CHUNK + (CHUNK - 1 if fwd else 0)
            tot = gc[edge:edge + 1, :]
            totals.append(tot)
            k_dec_parts.append(k[c * CHUNK:(c + 1) * CHUNK] * jnp.exp(tot - gc[c * CHUNK:(c + 1) * CHUNK]))
        k_dec_t = jnp.concatenate(k_dec_parts, axis=0).T.astype(BF16)

        state = s_ref[direction]
        outs = [None] * n_chunks
        for c in (range(n_chunks) if fwd else reversed(range(n_chunks))):
            r0, r1 = c * CHUNK, (c + 1) * CHUNK
            wq = jnp.concatenate([w[r0:r1], q_dec[r0:r1]], axis=0).astype(BF16)
            res = _dot(wq, state.astype(BF16))
            v_new = u[r0:r1] - res[:CHUNK]
            pieces = []
            if r0:
                pieces.append(jnp.zeros((r0, HEAD_DIM), F32))
            pieces.append(v_new)
            if GROUP - r1:
                pieces.append(jnp.zeros((GROUP - r1, HEAD_DIM), F32))
            v_pad = jnp.concatenate(pieces, axis=0).astype(BF16)
            outs[c] = res[CHUNK:] + _dot(intra[r0:r1], v_pad)
            state = state * jnp.exp(totals[c]) + _dot(k_dec_t, v_pad)
        s_ref[direction] = state
        o_grp = jnp.concatenate(outs, axis=0)
        if fwd:
            oacc_ref[pl.ds(g0, GROUP), :] = o_grp
        else:
            oacc_ref[pl.ds(g0, GROUP), :] += o_grp

    for direction in (0, 1):
        if has_s0:
            s_ref[direction] = s0_ref[direction]
        else:
            s_ref[direction] = jnp.zeros((HEAD_DIM, HEAD_DIM), F32)

    def fwd_body(gi, carry):
        scan_group(gi, 0)
        return carry

    def bwd_body(t, carry):
        scan_group(n_groups - 1 - t, 1)
        return carry

    lax.fori_loop(0, n_groups, fwd_body, 0)
    lax.fori_loop(0, n_groups, bwd_body, 0)

    if emit_state:
        sf_ref[...] = s_ref[...]

    def finish(gi, carry):
        g0 = pl.multiple_of(gi * GROUP, GROUP)
        o = oacc_ref[pl.ds(g0, GROUP), :]
        y = o * lax.rsqrt(jnp.mean(o * o, axis=-1, keepdims=True) + EPS) * gw_ref[...]
        o_ref[pl.ds(g0, GROUP), :] = (y * _silu(z_ref[pl.ds(g0, GROUP), :])).astype(o_ref.dtype)
        return carry

    lax.fori_loop(0, n_groups, finish, 0)


def _gdn(p_gdn, gate_col, gate_row, conv_w, a_log, dt_bias, norm_w, *, row0, n_seq, seq_len,
         heads, rope_tabs=None, s0=None, layer=None, emit_state=False):
    assert row0 % seq_len == 0 and seq_len % GROUP == 0
    sb0 = row0 // seq_len
    gb0 = row0 // GROUP
    gps = seq_len // GROUP
    rotary = rope_tabs is not None
    smem = pl.BlockSpec(memory_space=pltpu.SMEM)

    def col(off):
        return pl.BlockSpec((seq_len, HEAD_DIM), lambda s, h: (sb0 + s, off + h))

    def cw(off):
        return pl.BlockSpec((CONV_K, HEAD_DIM), lambda s, h: (0, off + h))

    in_specs = [smem, smem, col(0), col(heads), col(2 * heads), col(3 * heads),
                cw(0), cw(heads), cw(2 * heads),
                pl.BlockSpec((None, gps, GROUP, 4), lambda s, h: (h, gb0 // gps + s, 0, 0)),
                pl.BlockSpec((None, gps, 4, GROUP), lambda s, h: (h, gb0 // gps + s, 0, 0)),
                pl.BlockSpec((1, HEAD_DIM), lambda s, h: (0, 0))]
    args = [a_log, dt_bias, p_gdn, p_gdn, p_gdn, p_gdn, conv_w, conv_w, conv_w,
            gate_col, gate_row, norm_w.reshape(1, HEAD_DIM)]
    if rotary:
        in_specs += [pl.BlockSpec((seq_len, HEAD_DIM), lambda s, h: (0, 0))] * 2
        args += list(rope_tabs)
    if s0 is not None:
        in_specs.append(pl.BlockSpec((None, None, 2, None, HEAD_DIM, HEAD_DIM),
                                     lambda s, h: (s, layer, 0, h, 0, 0)))
        args.append(s0)
    out_shape = [jax.ShapeDtypeStruct((n_seq * seq_len, heads * HEAD_DIM), BF16)]
    out_specs = [pl.BlockSpec((seq_len, HEAD_DIM), lambda s, h: (s, h))]
    if emit_state:
        out_shape.append(jax.ShapeDtypeStruct((n_seq, 2, heads, HEAD_DIM, HEAD_DIM), F32))
        out_specs.append(pl.BlockSpec((None, 2, None, HEAD_DIM, HEAD_DIM), lambda s, h: (s, 0, h, 0, 0)))
    scratch = [pltpu.VMEM((seq_len + 2 * CONV_PAD, HEAD_DIM), F32)] * 3
    scratch += [pltpu.VMEM((seq_len, HEAD_DIM), F32)] * 4
    scratch += [pltpu.VMEM((2, HEAD_DIM, HEAD_DIM), F32)]
    res = pl.pallas_call(
        functools.partial(_gdn_kernel, seq_len=seq_len, rotary=rotary, has_s0=s0 is not None,
                          emit_state=emit_state),
        grid=(n_seq, heads),
        in_specs=in_specs, out_specs=out_specs, out_shape=out_shape, scratch_shapes=scratch,
        compiler_params=_cparams(("parallel", "arbitrary"), 24 * seq_len * HEAD_DIM * 4 + (16 << 20)),
        name="gdn_latent" if rotary else "gdn_context",
    )(*args)
    return res if emit_state else res[0]


def _head_rms(x, w):
    return x * lax.rsqrt(jnp.mean(x * x, axis=-1, keepdims=True) + EPS) * w


def _softmax_rows(parts):
    m = None
    for s in parts:
        mi = jnp.max(s, axis=-1, keepdims=True)
        m = mi if m is None else jnp.maximum(m, mi)
    ex = [jnp.exp(s - m) for s in parts]
    den = None
    for e in ex:
        si = jnp.sum(e, axis=-1, keepdims=True)
        den = si if den is None else den + si
    inv = 1.0 / den
    return [e * inv for e in ex]


def _ctx_attn_kernel(q_ref, k_ref, v_ref, qw_ref, kw_ref, nk_ref, o_ref):
    q = _head_rms(q_ref[...], qw_ref[...])
    k = _head_rms(k_ref[...], kw_ref[...])
    nk_ref[...] = k
    s = _dot_nt(q.astype(BF16), k.astype(BF16)) * (HEAD_DIM ** -0.5)
    (p,) = _softmax_rows([s])
    o_ref[...] = _dot(p.astype(BF16), v_ref[...].astype(BF16)).astype(o_ref.dtype)


def _ctx_attention(p_na, q_w, k_w, *, n_seq, seq_len, heads):
    def col(off):
        return pl.BlockSpec((seq_len, HEAD_DIM), lambda s, h: (s, off + h))

    wspec = pl.BlockSpec((1, HEAD_DIM), lambda s, h: (0, 0))
    rows = n_seq * seq_len
    return pl.pallas_call(
        _ctx_attn_kernel,
        grid=(n_seq, heads),
        in_specs=[col(0), col(heads), col(2 * heads), wspec, wspec],
        out_specs=[col(0), col(0)],
        out_shape=[jax.ShapeDtypeStruct((rows, heads * HEAD_DIM), F32),
                   jax.ShapeDtypeStruct((rows, heads * HEAD_DIM), BF16)],
        compiler_params=_cparams(("parallel", "arbitrary"), 16 << 20),
        name="ctx_attention",
    )(p_na, p_na, p_na, q_w.reshape(1, HEAD_DIM), k_w.reshape(1, HEAD_DIM))


def _bias_kernel(rpb_ref, o_ref, *, win_rows):
    head = pl.program_id(0)
    n_dr = 2 * NA_ROWS - 1
    n_dc = 2 * NA_COLS - 1
    qc = lax.broadcasted_iota(jnp.int32, (GRID_W, GRID_W), 0)
    kc = lax.broadcasted_iota(jnp.int32, (GRID_W, GRID_W), 1)
    dc = jnp.clip(kc - qc, -(NA_COLS - 1), NA_COLS - 1) + (NA_COLS - 1)
    start = jnp.clip(qc - NA_COLS // 2, 0, GRID_W - NA_COLS)
    in_win = (kc >= start) & (kc < start + NA_COLS)
    blocks = {}
    for e in range(win_rows):
        for j in range(win_rows):
            dr = j - e + (NA_ROWS - 1)
            if dr in blocks:
                continue
            acc = jnp.zeros((GRID_W, GRID_W), F32)
            for d in range(n_dc):
                acc = jnp.where(dc == d, rpb_ref[(head * n_dr + dr) * n_dc + d], acc)
            blocks[dr] = jnp.where(in_win, acc, -jnp.inf)
    for e in range(win_rows):
        o_ref[e] = jnp.concatenate([blocks[j - e + (NA_ROWS - 1)] for j in range(win_rows)], axis=1)


def _bias_table(rpb, win_rows):
    heads = rpb.shape[0]
    return pl.pallas_call(
        functools.partial(_bias_kernel, win_rows=win_rows),
        grid=(heads,),
        in_specs=[pl.BlockSpec(memory_space=pltpu.SMEM)],
        out_specs=pl.BlockSpec((None, win_rows, GRID_W, win_rows * GRID_W), lambda h: (h, 0, 0, 0)),
        out_shape=jax.ShapeDtypeStruct((heads, win_rows, GRID_W, win_rows * GRID_W), F32),
        compiler_params=_cparams(("arbitrary",), 16 << 20),
        name="na_bias",
    )(rpb.reshape(-1))


def _lat_attn_kernel(q_ref, k_ref, v_ref, ck_ref, cv_ref, qw_ref, kw_ref, bias_ref, o_ref,
                     qn_ref, kn_ref, vb_ref, *, seq_len, win_rows):
    rows = seq_len // GRID_W
    n_blk = seq_len // GROUP
    band = win_rows * GRID_W
    scale = HEAD_DIM ** -0.5

    def prep(bi, carry):
        r0 = pl.multiple_of(bi * GROUP, GROUP)
        qn_ref[pl.ds(r0, GROUP), :] = _head_rms(q_ref[pl.ds(r0, GROUP), :], qw_ref[...]).astype(BF16)
        kn_ref[pl.ds(r0, GROUP), :] = _head_rms(k_ref[pl.ds(r0, GROUP), :], kw_ref[...]).astype(BF16)
        vb_ref[pl.ds(r0, GROUP), :] = v_ref[pl.ds(r0, GROUP), :].astype(BF16)
        return carry

    lax.fori_loop(0, n_blk, prep, 0)
    ck = ck_ref[...].astype(BF16)
    cv = cv_ref[...].astype(BF16)

    def body(r, carry):
        rs = jnp.clip(r - win_rows // 2, 0, rows - win_rows)
        q0 = pl.multiple_of(r * GRID_W, GRID_W)
        k0 = pl.multiple_of(rs * GRID_W, GRID_W)
        q = qn_ref[pl.ds(q0, GRID_W), :]
        s_loc = _dot_nt(q, kn_ref[pl.ds(k0, band), :]) * scale + bias_ref[r - rs]
        s_ctx = _dot_nt(q, ck) * scale
        p_loc, p_ctx = _softmax_rows([s_loc, s_ctx])
        o = _dot(p_loc.astype(BF16), vb_ref[pl.ds(k0, band), :]) + _dot(p_ctx.astype(BF16), cv)
        o_ref[pl.ds(q0, GRID_W), :] = o.astype(o_ref.dtype)
        return carry

    lax.fori_loop(0, rows, body, 0)


def _lat_attention(p_na, cache_k, cache_v, q_w, k_w, bias, *, row0, n_seq, seq_len, heads, layer):
    assert row0 % seq_len == 0 and seq_len % GROUP == 0
    sb0 = row0 // seq_len
    past = cache_k.shape[2]
    win_rows = bias.shape[1]

    def col(off):
        return pl.BlockSpec((seq_len, HEAD_DIM), lambda s, h: (sb0 + s, off + h))

    cache = pl.BlockSpec((None, None, past, HEAD_DIM), lambda s, h: (s, layer, 0, h))
    wspec = pl.BlockSpec((1, HEAD_DIM), lambda s, h: (0, 0))
    return pl.pallas_call(
        functools.partial(_lat_attn_kernel, seq_len=seq_len, win_rows=win_rows),
        grid=(n_seq, heads),
        in_specs=[col(0), col(heads), col(2 * heads), cache, cache, wspec, wspec,
                  pl.BlockSpec((None, win_rows, GRID_W, win_rows * GRID_W), lambda s, h: (h, 0, 0, 0))],
        out_specs=pl.BlockSpec((seq_len, HEAD_DIM), lambda s, h: (s, h)),
        out_shape=jax.ShapeDtypeStruct((n_seq * seq_len, heads * HEAD_DIM), BF16),
        scratch_shapes=[pltpu.VMEM((seq_len, HEAD_DIM), BF16)] * 3,
        compiler_params=_cparams(("parallel", "arbitrary"), 10 * seq_len * HEAD_DIM * 4 + (16 << 20)),
        name="lat_attention",
    )(p_na, p_na, p_na, cache_k.reshape(cache_k.shape[:3] + (-1,)), cache_v.reshape(cache_v.shape[:3] + (-1,)),
      q_w.reshape(1, HEAD_DIM), k_w.reshape(1, HEAD_DIM), bias)


def _fourier_kernel(x_ref, ch_hi_ref, ch_lo_ref, pos_hi_ref, pos_lo_ref, o_ref):
    a = _dot3(_split2(x_ref[...]), (ch_hi_ref[...], ch_lo_ref[...]))
    stacked = jnp.concatenate([a[:, :HEAD_DIM], a[:, HEAD_DIM:]], axis=0)
    y = _dot3((pos_hi_ref[...], pos_lo_ref[...]), _split2(stacked))
    o_ref[...] = y.astype(o_ref.dtype)


def _dft_tables(seq_len):
    def cs(n):
        idx = jnp.arange(n, dtype=jnp.int32)
        ang = ((idx[:, None] * idx[None, :]) % n).astype(F32) * (2.0 * math.pi / n)
        return jnp.cos(ang), jnp.sin(ang)

    cc, sc = cs(HEAD_DIM)
    ct, st = cs(seq_len)
    chan = jnp.concatenate([cc, sc], axis=1) * (HEAD_DIM ** -0.5)
    pos = jnp.concatenate([ct, -st], axis=1) * (seq_len ** -0.5)
    return _split2(chan) + _split2(pos)


def _fourier(p_fn, *, row0, n_seq, seq_len, groups):
    assert row0 % seq_len == 0
    sb0 = row0 // seq_len
    ch_hi, ch_lo, pos_hi, pos_lo = _dft_tables(seq_len)
    tr = _tile(seq_len, 512, 8)
    n_rt = seq_len // tr
    chan = pl.BlockSpec((HEAD_DIM, 2 * HEAD_DIM), lambda r, s, g: (0, 0))
    pos = pl.BlockSpec((tr, 2 * seq_len), lambda r, s, g: (r, 0))
    return pl.pallas_call(
        _fourier_kernel,
        grid=(n_rt, n_seq, groups),
        in_specs=[pl.BlockSpec((seq_len, HEAD_DIM), lambda r, s, g: (sb0 + s, g)), chan, chan, pos, pos],
        out_specs=pl.BlockSpec((tr, HEAD_DIM), lambda r, s, g: (s * n_rt + r, g)),
        out_shape=jax.ShapeDtypeStruct((n_seq * seq_len, groups * HEAD_DIM), BF16),
        compiler_params=_cparams(("arbitrary", "arbitrary", "arbitrary"),
                                 16 * tr * seq_len + 40 * seq_len * HEAD_DIM + (8 << 20)),
        name="fourier",
    )(p_fn, ch_hi, ch_lo, pos_hi, pos_lo)


def _rope_tables(seq_len):
    half = HEAD_DIM // 2
    t = jnp.arange(seq_len)
    inv_freq = 1.0 / (ROPE_BASE ** (jnp.arange(0, half, 2, dtype=F32) / half))
    ang_row = (t // GRID_W).astype(F32)[:, None] * inv_freq
    ang_col = (t % GRID_W).astype(F32)[:, None] * inv_freq
    cos = jnp.concatenate([jnp.cos(ang_row)] * 2 + [jnp.cos(ang_col)] * 2, axis=1)
    sin = jnp.concatenate([-jnp.sin(ang_row), jnp.sin(ang_row), -jnp.sin(ang_col), jnp.sin(ang_col)], axis=1)
    return cos, sin


def _gate_layouts(p_ab, heads):
    nt = p_ab.shape[0]
    per_head = p_ab[:, :4 * heads].reshape(nt // GROUP, GROUP, 4, heads)
    return per_head.transpose(3, 0, 1, 2), per_head.transpose(3, 0, 2, 1)


def kernel(x_prompt, x_sample, cache_na_k, cache_na_v, state_gdn, c, c_ctx, norm1_w, norm2_w, ada_w, ada_b, in_w, conv_w, gdn_a_log, gdn_dt_bias, gdn_norm_w, na_q_norm_w, na_k_norm_w, na_rpb, out_w, ffn_gate_w, ffn_up_w, ffn_down_w):
    batch, seq, d = x_prompt.shape
    dec_batch, dec_seq, _ = x_sample.shape
    depth = in_w.shape[0]
    gh = state_gdn.shape[3]
    nh = cache_na_k.shape[3]
    gw, nw = gh * HEAD_DIM, nh * HEAD_DIM
    fw = in_w.shape[2] - 4 * gw - 4 * gh - 3 * nw
    fg = fw // HEAD_DIM
    n_ctx, n_lat = batch * seq, dec_batch * dec_seq
    off_ab = 4 * gw
    off_na = off_ab + 4 * gh
    off_fn = off_na + 3 * nw
    assert dec_batch + 1 <= MOD_ROWS and 4 * gh <= HEAD_DIM
    win_rows = min(NA_ROWS, dec_seq // GRID_W)

    cond = jnp.concatenate([c_ctx[None, :], c, jnp.zeros((MOD_ROWS - 1 - dec_batch, d), F32)], axis=0)
    mods_all = _adaln(cond, ada_w, ada_b)
    rope_tabs = _rope_tables(dec_seq)

    x = jnp.concatenate([x_prompt.reshape(n_ctx, d), x_sample.reshape(n_lat, d)], axis=0)
    new_k, new_v, new_s = [], [], []
    for l in range(depth):
        mods = mods_all[l].reshape(MOD_ROWS * 6, 1, d)
        w_in = in_w[l]
        w_gdn = w_in[:, :off_ab].astype(BF16)
        w_ab = jnp.pad(w_in[:, off_ab:off_na], ((0, 0), (0, HEAD_DIM - 4 * gh))).astype(BF16)
        w_na = w_in[:, off_na:off_fn].astype(BF16)
        w_fn = w_in[:, off_fn:].astype(BF16)

        h = _modulate(x, norm1_w[l], mods, 0, 1, n_ctx, dec_seq)
        p_gdn = _matmul(h, w_gdn, F32, "proj_gdn")
        p_ab = _matmul(h, w_ab, F32, "proj_gates")
        p_na = _matmul(h, w_na, F32, "proj_na")
        p_fn = _matmul(h, w_fn, F32, "proj_fn")

        gate_col, gate_row = _gate_layouts(p_ab, gh)
        gdn_common = (p_gdn, gate_col, gate_row, conv_w[l], gdn_a_log[l], gdn_dt_bias[l], gdn_norm_w[l])
        gdn_ctx, s_ctx = _gdn(*gdn_common, row0=0, n_seq=batch, seq_len=seq, heads=gh, emit_state=True)
        gdn_lat = _gdn(*gdn_common, row0=n_ctx, n_seq=dec_batch, seq_len=dec_seq, heads=gh,
                       rope_tabs=rope_tabs, s0=state_gdn, layer=l)

        nk_ctx, na_ctx = _ctx_attention(p_na, na_q_norm_w[l], na_k_norm_w[l], n_seq=batch, seq_len=seq, heads=nh)
        bias = _bias_table(na_rpb[l], win_rows)
        na_lat = _lat_attention(p_na, cache_na_k, cache_na_v, na_q_norm_w[l], na_k_norm_w[l], bias,
                                row0=n_ctx, n_seq=dec_batch, seq_len=dec_seq, heads=nh, layer=l)

        fn_ctx = _fourier(p_fn, row0=0, n_seq=batch, seq_len=seq, groups=fg)
        fn_lat = _fourier(p_fn, row0=n_ctx, n_seq=dec_batch, seq_len=dec_seq, groups=fg)

        parts = [jnp.concatenate([gdn_ctx, gdn_lat], axis=0),
                 jnp.concatenate([na_ctx, na_lat], axis=0),
                 jnp.concatenate([fn_ctx, fn_lat], axis=0)]
        w_out = out_w[l].astype(BF16)
        x = _outproj(parts, [w_out[:gw], w_out[gw:gw + nw], w_out[gw + nw:]], x, mods, 2, n_ctx, dec_seq)

        h2 = _modulate(x, norm2_w[l], mods, 3, 4, n_ctx, dec_seq)
        act = _swiglu(h2, ffn_gate_w[l].astype(BF16), ffn_up_w[l].astype(BF16))
        x = _down(act, ffn_down_w[l].astype(BF16), x, mods, 5, n_ctx, dec_seq)

        new_k.append(nk_ctx.reshape(batch, seq, nh, HEAD_DIM))
        new_v.append(p_na[:n_ctx, 2 * nw:].reshape(batch, seq, nh, HEAD_DIM))
        new_s.append(s_ctx)

    y_prompt = x[:n_ctx].reshape(batch, seq, d)
    y_sample = x[n_ctx:].reshape(dec_batch, dec_seq, d)
    return (y_prompt, y_sample, jnp.stack(new_k, axis=1), jnp.stack(new_v, axis=1), jnp.stack(new_s, axis=1))
```

```python
import functools
import math

import jax
import jax.numpy as jnp
from jax import lax
from jax.experimental import pallas as pl
from jax.experimental.pallas import tpu as pltpu

F32 = jnp.float32
BF16 = jnp.bfloat16

HEAD_DIM = 128
CONV_K = 5
CHUNK = 64
GROUP = 256
GRID_W = 64
NA_ROWS = 8
NA_COLS = 16
ROPE_BASE = 10000.0
EPS = 1e-6
MOD_ROWS = 8
CONV_PAD = 8
V7X_VMEM_BYTES = 64 * 1024 * 1024
VMEM_CAP = V7X_VMEM_BYTES - 8 * 1024 * 1024


def _cparams(sem, vmem_bytes):
    limit = int(min(max(vmem_bytes, 16 * 1024 * 1024), VMEM_CAP))
    return pltpu.CompilerParams(dimension_semantics=sem, vmem_limit_bytes=limit)


def _tile(n, pref, align):
    if n <= pref:
        return n
    t = (pref // align) * align
    while t >= align:
        if n % t == 0:
            return t
        t -= align
    return n


def _sigmoid(x):
    return 1.0 / (1.0 + jnp.exp(-x))


def _silu(x):
    return x * _sigmoid(x)


def _softplus(x):
    return jnp.maximum(x, 0.0) + jnp.log(1.0 + jnp.exp(-jnp.abs(x)))


def _dot(a, b):
    return jnp.dot(a, b, preferred_element_type=F32)


def _dot_nt(a, b):
    return lax.dot_general(a, b, (((1,), (1,)), ((), ())), preferred_element_type=F32)


def _adaln_kernel(c_ref, w_ref, b_ref, o_ref):
    s = _silu(c_ref[...]).astype(BF16)
    o_ref[...] = _dot(s, w_ref[...].astype(BF16)) + b_ref[...]


def _adaln(cond, ada_w, ada_b):
    depth, d, n = ada_w.shape
    tn = _tile(n, 512, 128)
    return pl.pallas_call(
        _adaln_kernel,
        grid=(depth, n // tn),
        in_specs=[pl.BlockSpec((MOD_ROWS, d), lambda l, j: (0, 0)),
                  pl.BlockSpec((None, d, tn), lambda l, j: (l, 0, j)),
                  pl.BlockSpec((None, 1, tn), lambda l, j: (l, 0, j))],
        out_specs=pl.BlockSpec((None, MOD_ROWS, tn), lambda l, j: (l, 0, j)),
        out_shape=jax.ShapeDtypeStruct((depth, MOD_ROWS, n), F32),
        compiler_params=_cparams(("parallel", "arbitrary"), 3 * d * tn * 4 + (4 << 20)),
        name="adaln",
    )(cond, ada_w, ada_b.reshape(depth, 1, n))


class _Rows:
    def __init__(self, n_ctx, lat_seq, tm):
        assert n_ctx % tm == 0 and lat_seq % tm == 0
        self.ctx_tiles = n_ctx // tm
        self.seq_tiles = lat_seq // tm

    def mod_index(self, i, which):
        row = jnp.where(i < self.ctx_tiles, 0, 1 + (i - self.ctx_tiles) // self.seq_tiles)
        return row * 6 + which


def _modulate_kernel(x_ref, g_ref, sh_ref, sc_ref, o_ref):
    x = x_ref[...]
    y = x * lax.rsqrt(jnp.mean(x * x, axis=-1, keepdims=True) + EPS) * g_ref[...]
    o_ref[...] = (y * (1.0 + sc_ref[...]) + sh_ref[...]).astype(o_ref.dtype)


def _modulate(x, gain, mods, shift_id, scale_id, n_ctx, lat_seq):
    nt, d = x.shape
    tm = _tile(math.gcd(n_ctx, lat_seq), 256, 8)
    rows = _Rows(n_ctx, lat_seq, tm)
    return pl.pallas_call(
        _modulate_kernel,
        grid=(nt // tm,),
        in_specs=[pl.BlockSpec((tm, d), lambda i: (i, 0)),
                  pl.BlockSpec((1, d), lambda i: (0, 0)),
                  pl.BlockSpec((None, 1, d), lambda i: (rows.mod_index(i, shift_id), 0, 0)),
                  pl.BlockSpec((None, 1, d), lambda i: (rows.mod_index(i, scale_id), 0, 0))],
        out_specs=pl.BlockSpec((tm, d), lambda i: (i, 0)),
        out_shape=jax.ShapeDtypeStruct((nt, d), BF16),
        compiler_params=_cparams(("parallel",), 16 * tm * d + (4 << 20)),
        name="modulate",
    )(x, gain.reshape(1, d), mods, mods)


def _mm_kernel(a_ref, w_ref, o_ref):
    o_ref[...] = _dot(a_ref[...], w_ref[...].astype(BF16)).astype(o_ref.dtype)


def _matmul(a, w, out_dtype, name, layer=None, n_cols=None):
    m, k = a.shape
    n = w.shape[-1] if n_cols is None else n_cols
    tm = _tile(m, 1024, 8)
    tn = _tile(n, 512, 128)
    if layer is None:
        w_spec = pl.BlockSpec((k, tn), lambda i, j: (0, j))
    else:
        w_spec = pl.BlockSpec((None, k, tn), lambda i, j: (layer, 0, j))
    wbytes = jnp.dtype(w.dtype).itemsize
    return pl.pallas_call(
        _mm_kernel,
        grid=(m // tm, n // tn),
        in_specs=[pl.BlockSpec((tm, k), lambda i, j: (i, 0)), w_spec],
        out_specs=pl.BlockSpec((tm, tn), lambda i, j: (i, j)),
        out_shape=jax.ShapeDtypeStruct((m, n), out_dtype),
        compiler_params=_cparams(("parallel", "arbitrary"),
                                 4 * tm * k + (2 * wbytes + 2) * k * tn + 12 * tm * tn + (4 << 20)),
        name=name,
    )(a, w)


def _outproj_kernel(*refs, n_parts, widths, kb, ctx_tiles):
    ctx_refs = refs[:n_parts]
    lat_refs = refs[n_parts:2 * n_parts]
    n_w = sum(widths) // kb
    w_refs = refs[2 * n_parts:2 * n_parts + n_w]
    x_ref, g_ref, o_ref = refs[2 * n_parts + n_w:]

    def run(a_refs):
        acc = None
        blk = 0
        for a_ref, width in zip(a_refs, widths):
            for c in range(width // kb):
                term = _dot(a_ref[:, c * kb:(c + 1) * kb], w_refs[blk][...])
                acc = term if acc is None else acc + term
                blk += 1
        o_ref[...] = x_ref[...] + g_ref[...] * acc

    is_ctx = pl.program_id(0) < ctx_tiles

    @pl.when(is_ctx)
    def _():
        run(ctx_refs)

    @pl.when(jnp.logical_not(is_ctx))
    def _():
        run(lat_refs)


def _outproj(ctx_parts, lat_parts, w_out, x, mods, gate_id, n_ctx, lat_seq):
    nt, d = x.shape
    tm = _tile(math.gcd(n_ctx, lat_seq), 1024, 8)
    tn = _tile(d, 512, 128)
    rows = _Rows(n_ctx, lat_seq, tm)
    widths = tuple(p.shape[1] for p in ctx_parts)
    kb = functools.reduce(math.gcd, widths)
    ktot = sum(widths)
    ct = rows.ctx_tiles
    lt = (nt - n_ctx) // tm
    once = pl.Buffered(1)
    a_specs = [pl.BlockSpec((tm, w), lambda i, j: (jnp.minimum(i, ct - 1), 0), pipeline_mode=once)
               for w in widths]
    a_specs += [pl.BlockSpec((tm, w), lambda i, j: (jnp.clip(i - ct, 0, lt - 1), 0), pipeline_mode=once)
                for w in widths]
    w_specs = [pl.BlockSpec((kb, tn), lambda i, j, b=b: (b, j)) for b in range(ktot // kb)]
    return pl.pallas_call(
        functools.partial(_outproj_kernel, n_parts=len(widths), widths=widths, kb=kb, ctx_tiles=ct),
        grid=(nt // tm, d // tn),
        in_specs=a_specs + w_specs + [
            pl.BlockSpec((tm, tn), lambda i, j: (i, j)),
            pl.BlockSpec((None, 1, tn), lambda i, j: (rows.mod_index(i, gate_id), 0, j))],
        out_specs=pl.BlockSpec((tm, tn), lambda i, j: (i, j)),
        out_shape=jax.ShapeDtypeStruct((nt, d), F32),
        compiler_params=_cparams(("parallel", "arbitrary"),
                                 4 * tm * ktot + 4 * ktot * tn + 24 * tm * tn + (4 << 20)),
        name="outproj",
    )(*ctx_parts, *lat_parts, *([w_out] * (ktot // kb)), x, mods)


def _swiglu_kernel(h_ref, wg_ref, wu_ref, o_ref):
    h = h_ref[...]
    g = _dot(h, wg_ref[...].astype(BF16))
    u = _dot(h, wu_ref[...].astype(BF16))
    o_ref[...] = (_silu(g) * u).astype(o_ref.dtype)


def _swiglu(h, wg, wu, layer):
    m, k = h.shape
    n = wg.shape[2]
    tm = _tile(m, 2048, 8)
    tn = _tile(n, 256, 128)
    w_spec = pl.BlockSpec((None, k, tn), lambda i, j: (layer, 0, j))
    return pl.pallas_call(
        _swiglu_kernel,
        grid=(m // tm, n // tn),
        in_specs=[pl.BlockSpec((tm, k), lambda i, j: (i, 0), pipeline_mode=pl.Buffered(1)), w_spec, w_spec],
        out_specs=pl.BlockSpec((tm, tn), lambda i, j: (i, j)),
        out_shape=jax.ShapeDtypeStruct((m, n), BF16),
        compiler_params=_cparams(("parallel", "arbitrary"),
                                 2 * tm * k + 20 * k * tn + 24 * tm * tn + (4 << 20)),
        name="swiglu",
    )(h, wg, wu)


def _down_kernel(a_ref, w_ref, x_ref, g_ref, o_ref, acc_ref, *, k_steps):
    kk = pl.program_id(2)
    part = _dot(a_ref[...], w_ref[...])

    @pl.when(kk == 0)
    def _():
        acc_ref[...] = part

    @pl.when(kk > 0)
    def _():
        acc_ref[...] += part

    @pl.when(kk == k_steps - 1)
    def _():
        o_ref[...] = x_ref[...] + g_ref[...] * acc_ref[...]


def _down(a, w, x, mods, gate_id, n_ctx, lat_seq):
    nt, k = a.shape
    d = w.shape[1]
    tm = _tile(math.gcd(n_ctx, lat_seq), 1024, 8)
    tn = _tile(d, 512, 128)
    tk = _tile(k, 6144, 128)
    rows = _Rows(n_ctx, lat_seq, tm)
    return pl.pallas_call(
        functools.partial(_down_kernel, k_steps=k // tk),
        grid=(nt // tm, d // tn, k // tk),
        in_specs=[pl.BlockSpec((tm, tk), lambda i, j, s: (i, s)),
                  pl.BlockSpec((tk, tn), lambda i, j, s: (s, j)),
                  pl.BlockSpec((tm, tn), lambda i, j, s: (i, j)),
                  pl.BlockSpec((None, 1, tn), lambda i, j, s: (rows.mod_index(i, gate_id), 0, j))],
        out_specs=pl.BlockSpec((tm, tn), lambda i, j, s: (i, j)),
        out_shape=jax.ShapeDtypeStruct((nt, d), F32),
        scratch_shapes=[pltpu.VMEM((tm, tn), F32)],
        compiler_params=_cparams(("parallel", "arbitrary", "arbitrary"),
                                 4 * tm * tk + 4 * tk * tn + 28 * tm * tn + (4 << 20)),
        name="down",
    )(a, w, x, mods)


def _neumann_solve(strict, rhs):
    def split2(v):
        hi = v.astype(BF16)
        return hi, (v - hi.astype(F32)).astype(BF16)

    def dot3(a, b):
        return _dot(a[0], b[0]) + _dot(a[0], b[1]) + _dot(a[1], b[0])

    p = -strict
    x = rhs
    n_fac = int(math.log2(CHUNK))
    for f in range(n_fac):
        ps = split2(p)
        x = x + dot3(ps, split2(x))
        if f < n_fac - 1:
            p = dot3(ps, ps)
    return x


def _gdn_kernel(*refs, seq_len, rotary, has_s0, emit_state):
    it = iter(refs)
    alog_ref, dtb_ref = next(it), next(it)
    q_ref, k_ref, v_ref, z_ref = next(it), next(it), next(it), next(it)
    cw_refs = (next(it), next(it), next(it))
    acol_ref, arow_ref, gw_ref = next(it), next(it), next(it)
    cos_ref = sin_ref = s0_ref = sf_ref = None
    if rotary:
        cos_ref, sin_ref = next(it), next(it)
    if has_s0:
        s0_ref = next(it)
    o_ref = next(it)
    if emit_state:
        sf_ref = next(it)
    pad_refs = (next(it), next(it), next(it))
    qs_ref, kn_ref, vc_ref, oacc_ref, s_ref = next(it), next(it), next(it), next(it), next(it)

    head = pl.program_id(1)
    n_groups = seq_len // GROUP
    n_chunks = GROUP // CHUNK
    scale = HEAD_DIM ** -0.5

    for src, pad in zip((q_ref, k_ref, v_ref), pad_refs):
        pad[pl.ds(0, CONV_PAD), :] = jnp.zeros((CONV_PAD, HEAD_DIM), F32)
        pad[pl.ds(CONV_PAD + seq_len, CONV_PAD), :] = jnp.zeros((CONV_PAD, HEAD_DIM), F32)
        pad[pl.ds(CONV_PAD, seq_len), :] = src[...]

    lane = lax.broadcasted_iota(jnp.int32, (GROUP, HEAD_DIM), 1)
    first_half = (lane % 64) < 32

    def conv_silu(pad, cw_ref, g0):
        win = pad[pl.ds(g0, GROUP + 2 * CONV_PAD), :]
        cw = cw_ref[...]
        acc = None
        for tap in range(CONV_K):
            off = CONV_PAD - CONV_K // 2 + tap
            term = win[off:off + GROUP, :] * cw[tap:tap + 1, :]
            acc = term if acc is None else acc + term
        return _silu(acc)

    def l2n(x):
        return x * lax.rsqrt(jnp.sum(x * x, axis=-1, keepdims=True) + EPS)

    def rope(x, g0):
        cos = cos_ref[pl.ds(g0, GROUP), :]
        sin = sin_ref[pl.ds(g0, GROUP), :]
        swapped = jnp.where(first_half, pltpu.roll(x, HEAD_DIM - 32, axis=1), pltpu.roll(x, 32, axis=1))
        return x * cos + swapped * sin

    def prep(gi):
        g0 = pl.multiple_of(gi * GROUP, GROUP)
        q = l2n(conv_silu(pad_refs[0], cw_refs[0], g0))
        k = l2n(conv_silu(pad_refs[1], cw_refs[1], g0))
        v = conv_silu(pad_refs[2], cw_refs[2], g0)
        if rotary:
            q, k = rope(q, g0), rope(k, g0)
        qs_ref[pl.ds(g0, GROUP), :] = q * scale
        kn_ref[pl.ds(g0, GROUP), :] = k
        vc_ref[pl.ds(g0, GROUP), :] = v

    ri = lax.broadcasted_iota(jnp.int32, (GROUP, GROUP), 0)
    ci = lax.broadcasted_iota(jnp.int32, (GROUP, GROUP), 1)
    same = (ri // CHUNK) == (ci // CHUNK)
    tok_chunk = lax.broadcasted_iota(jnp.int32, (HEAD_DIM, GROUP), 1) // CHUNK

    def solve_group(gi, direction):
        fwd = direction == 0
        g0 = pl.multiple_of(gi * GROUP, GROUP)
        qs = qs_ref[pl.ds(g0, GROUP), :]
        k = kn_ref[pl.ds(g0, GROUP), :]
        v = vc_ref[pl.ds(g0, GROUP), :]
        ac = acol_ref[gi]
        ar = arow_ref[gi]
        neg_ea_c = -jnp.exp(jnp.full((GROUP, 1), alog_ref[direction, head], F32))
        neg_ea_r = -jnp.exp(jnp.full((1, GROUP), alog_ref[direction, head], F32))
        dtb = dtb_ref[direction, head]
        g_c = neg_ea_c * _softplus(ac[:, direction:direction + 1] + dtb)
        g_r = neg_ea_r * _softplus(ar[direction:direction + 1, :] + dtb)
        beta = _sigmoid(ac[:, 2 + direction:3 + direction])

        incl = same & ((ci <= ri) if fwd else (ci >= ri))
        strict = same & ((ci < ri) if fwd else (ci > ri))
        ones_incl = jnp.where(incl, 1.0, 0.0).astype(BF16)
        gcb = jnp.broadcast_to(g_c, (GROUP, HEAD_DIM))
        grb = jnp.broadcast_to(g_r, (8, GROUP))
        cum_c = jnp.zeros((GROUP, HEAD_DIM), F32)
        cum_r = jnp.zeros((8, GROUP), F32)
        for _ in range(3):
            piece_c = gcb.astype(BF16)
            piece_r = grb.astype(BF16)
            cum_c = cum_c + _dot(ones_incl, piece_c)
            cum_r = cum_r + _dot_nt(piece_r, ones_incl)
            gcb = gcb - piece_c.astype(F32)
            grb = grb - piece_r.astype(F32)
        gc = cum_c[:, 0:1]
        gr = cum_r[0:1, :]

        decay = jnp.exp(jnp.where(incl, gc - gr, -jnp.inf))
        kb = k * beta
        both = _dot_nt(jnp.concatenate([kb, qs], axis=0).astype(BF16), k.astype(BF16))
        tri = jnp.where(strict, both[:GROUP] * decay, 0.0)
        intra = (both[GROUP:] * decay).astype(BF16)
        eg = jnp.exp(gc)
        sol = _neumann_solve(tri, jnp.concatenate([v * beta, kb * eg], axis=1))
        k_dec_parts = []
        chunk_decay = []
        for c in range(n_chunks):
            edge = c * CHUNK + (CHUNK - 1 if fwd else 0)
            tot = gc[edge:edge + 1, :]
            chunk_decay.append(jnp.exp(tot))
            k_dec_parts.append(k[c * CHUNK:(c + 1) * CHUNK] * jnp.exp(tot - gc[c * CHUNK:(c + 1) * CHUNK]))
        k_dec_t = jnp.concatenate(k_dec_parts, axis=0).T
        sol_b = sol.astype(BF16)
        kt_uw = [_dot(jnp.where(tok_chunk == c, k_dec_t, 0.0).astype(BF16), sol_b)
                 for c in range(n_chunks)]
        a_uw = _dot(intra, sol_b)
        q_eff = (qs * eg - a_uw[:, HEAD_DIM:]).astype(BF16)
        return q_eff, a_uw[:, :HEAD_DIM], kt_uw, chunk_decay

    def run_chain(direction, solved, state):
        q_eff, o_loc, kt_uw, chunk_decay = solved
        outs = [None] * n_chunks
        for c in (range(n_chunks) if direction == 0 else reversed(range(n_chunks))):
            r0, r1 = c * CHUNK, (c + 1) * CHUNK
            sb = state.astype(BF16)
            outs[c] = _dot(q_eff[r0:r1], sb) + o_loc[r0:r1]
            ktw = kt_uw[c][:, HEAD_DIM:].astype(BF16)
            state = state * chunk_decay[c] + kt_uw[c][:, :HEAD_DIM] - _dot(ktw, sb)
        return jnp.concatenate(outs, axis=0), state

    def finish(o, g0):
        y = o * lax.rsqrt(jnp.mean(o * o, axis=-1, keepdims=True) + EPS) * gw_ref[...]
        o_ref[pl.ds(g0, GROUP), :] = (y * _silu(z_ref[pl.ds(g0, GROUP), :])).astype(o_ref.dtype)

    def init_state(direction):
        if has_s0:
            return s0_ref[direction]
        return jnp.zeros((HEAD_DIM, HEAD_DIM), F32)

    if n_groups == 1:
        prep(0)
        o_f, s_f = run_chain(0, solve_group(0, 0), init_state(0))
        o_b, s_b = run_chain(1, solve_group(0, 1), init_state(1))
        if emit_state:
            sf_ref[0] = s_f
            sf_ref[1] = s_b
        finish(o_f + o_b, 0)
        return

    def prep_body(gi, carry):
        prep(gi)
        oacc_ref[pl.ds(pl.multiple_of(gi * GROUP, GROUP), GROUP), :] = jnp.zeros((GROUP, HEAD_DIM), F32)
        return carry

    lax.fori_loop(0, n_groups, prep_body, 0)
    s_ref[0] = init_state(0)
    s_ref[1] = init_state(1)

    def scan_body(t, carry):
        gf, gb = t, n_groups - 1 - t
        o_f, s_f = run_chain(0, solve_group(gf, 0), s_ref[0])
        o_b, s_b = run_chain(1, solve_group(gb, 1), s_ref[1])
        s_ref[0] = s_f
        s_ref[1] = s_b
        oacc_ref[pl.ds(pl.multiple_of(gf * GROUP, GROUP), GROUP), :] += o_f
        oacc_ref[pl.ds(pl.multiple_of(gb * GROUP, GROUP), GROUP), :] += o_b
        return carry

    lax.fori_loop(0, n_groups, scan_body, 0)
    if emit_state:
        sf_ref[...] = s_ref[...]

    def finish_body(gi, carry):
        g0 = pl.multiple_of(gi * GROUP, GROUP)
        finish(oacc_ref[pl.ds(g0, GROUP), :], g0)
        return carry

    lax.fori_loop(0, n_groups, finish_body, 0)


def _gdn(p_gdn, gate_col, gate_row, conv_w, a_log, dt_bias, norm_w, *, row0, n_seq, seq_len,
         heads, rope_tabs=None, s0=None, layer=None, emit_state=False):
    assert row0 % seq_len == 0 and seq_len % GROUP == 0
    sb0 = row0 // seq_len
    gps = seq_len // GROUP
    rotary = rope_tabs is not None
    smem = pl.BlockSpec(memory_space=pltpu.SMEM)

    def col(off):
        return pl.BlockSpec((seq_len, HEAD_DIM), lambda s, h: (sb0 + s, off + h))

    def cw(off):
        return pl.BlockSpec((CONV_K, HEAD_DIM), lambda s, h: (0, off + h))

    in_specs = [smem, smem, col(0), col(heads), col(2 * heads), col(3 * heads),
                cw(0), cw(heads), cw(2 * heads),
                pl.BlockSpec((None, gps, GROUP, 4), lambda s, h: (h, sb0 + s, 0, 0)),
                pl.BlockSpec((None, gps, 4, GROUP), lambda s, h: (h, sb0 + s, 0, 0)),
                pl.BlockSpec((1, HEAD_DIM), lambda s, h: (0, 0))]
    args = [a_log, dt_bias, p_gdn, p_gdn, p_gdn, p_gdn, conv_w, conv_w, conv_w,
            gate_col, gate_row, norm_w.reshape(1, HEAD_DIM)]
    if rotary:
        in_specs += [pl.BlockSpec((seq_len, HEAD_DIM), lambda s, h: (0, 0))] * 2
        args += list(rope_tabs)
    if s0 is not None:
        in_specs.append(pl.BlockSpec((None, None, 2, None, HEAD_DIM, HEAD_DIM),
                                     lambda s, h: (s, layer, 0, h, 0, 0)))
        args.append(s0)
    out_shape = [jax.ShapeDtypeStruct((n_seq * seq_len, heads * HEAD_DIM), BF16)]
    out_specs = [pl.BlockSpec((seq_len, HEAD_DIM), lambda s, h: (s, h))]
    if emit_state:
        out_shape.append(jax.ShapeDtypeStruct((n_seq, 2, heads, HEAD_DIM, HEAD_DIM), F32))
        out_specs.append(pl.BlockSpec((None, 2, None, HEAD_DIM, HEAD_DIM), lambda s, h: (s, 0, h, 0, 0)))
    scratch = [pltpu.VMEM((seq_len + 2 * CONV_PAD, HEAD_DIM), F32)] * 3
    scratch += [pltpu.VMEM((seq_len, HEAD_DIM), F32)] * 4
    scratch += [pltpu.VMEM((2, HEAD_DIM, HEAD_DIM), F32)]
    res = pl.pallas_call(
        functools.partial(_gdn_kernel, seq_len=seq_len, rotary=rotary, has_s0=s0 is not None,
                          emit_state=emit_state),
        grid=(n_seq, heads),
        in_specs=in_specs, out_specs=out_specs, out_shape=out_shape, scratch_shapes=scratch,
        compiler_params=_cparams(("parallel", "arbitrary"), 24 * seq_len * HEAD_DIM * 4 + (16 << 20)),
        name="gdn_latent" if rotary else "gdn_context",
    )(*args)
    return res if emit_state else res[0]


def _head_rms(x, w):
    return x * lax.rsqrt(jnp.mean(x * x, axis=-1, keepdims=True) + EPS) * w


def _softmax_rows(parts):
    m = None
    for s in parts:
        mi = jnp.max(s, axis=-1, keepdims=True)
        m = mi if m is None else jnp.maximum(m, mi)
    ex = [jnp.exp(s - m) for s in parts]
    den = None
    for e in ex:
        si = jnp.sum(e, axis=-1, keepdims=True)
        den = si if den is None else den + si
    inv = 1.0 / den
    return [e * inv for e in ex]


def _ctx_attn_kernel(q_ref, k_ref, v_ref, qw_ref, kw_ref, nk_ref, o_ref, *, heads_per_step):
    for j in range(heads_per_step):
        cols = slice(j * HEAD_DIM, (j + 1) * HEAD_DIM)
        q = _head_rms(q_ref[:, cols], qw_ref[...])
        k = _head_rms(k_ref[:, cols], kw_ref[...])
        nk_ref[:, cols] = k
        s = _dot_nt(q.astype(BF16), k.astype(BF16)) * (HEAD_DIM ** -0.5)
        (p,) = _softmax_rows([s])
        o_ref[:, cols] = _dot(p.astype(BF16), v_ref[:, cols].astype(BF16)).astype(o_ref.dtype)


def _ctx_attention(p_na, q_w, k_w, *, n_seq, seq_len, heads):
    hps = _tile(heads, 8, 1)
    hb = heads // hps
    width = hps * HEAD_DIM

    def col(off):
        return pl.BlockSpec((seq_len, width), lambda s, h: (s, off + h))

    wspec = pl.BlockSpec((1, HEAD_DIM), lambda s, h: (0, 0))
    rows = n_seq * seq_len
    return pl.pallas_call(
        functools.partial(_ctx_attn_kernel, heads_per_step=hps),
        grid=(n_seq, hb),
        in_specs=[col(0), col(hb), col(2 * hb), wspec, wspec],
        out_specs=[col(0), col(0)],
        out_shape=[jax.ShapeDtypeStruct((rows, heads * HEAD_DIM), F32),
                   jax.ShapeDtypeStruct((rows, heads * HEAD_DIM), BF16)],
        compiler_params=_cparams(("parallel", "arbitrary"), 40 * seq_len * width + (8 << 20)),
        name="ctx_attention",
    )(p_na, p_na, p_na, q_w.reshape(1, HEAD_DIM), k_w.reshape(1, HEAD_DIM))


def _bias_kernel(rpb_ref, o_ref, *, win_rows):
    head = pl.program_id(0)
    n_dr = 2 * NA_ROWS - 1
    n_dc = 2 * NA_COLS - 1
    qc = lax.broadcasted_iota(jnp.int32, (GRID_W, GRID_W), 0)
    kc = lax.broadcasted_iota(jnp.int32, (GRID_W, GRID_W), 1)
    dc = jnp.clip(kc - qc, -(NA_COLS - 1), NA_COLS - 1) + (NA_COLS - 1)
    start = jnp.clip(qc - NA_COLS // 2, 0, GRID_W - NA_COLS)
    in_win = (kc >= start) & (kc < start + NA_COLS)
    blocks = {}
    for e in range(win_rows):
        for j in range(win_rows):
            dr = j - e + (NA_ROWS - 1)
            if dr in blocks:
                continue
            acc = jnp.zeros((GRID_W, GRID_W), F32)
            for d in range(n_dc):
                acc = jnp.where(dc == d, rpb_ref[(head * n_dr + dr) * n_dc + d], acc)
            blocks[dr] = jnp.where(in_win, acc, -jnp.inf)
    for e in range(win_rows):
        o_ref[e] = jnp.concatenate([blocks[j - e + (NA_ROWS - 1)] for j in range(win_rows)], axis=1)


def _bias_table(rpb, win_rows):
    heads = rpb.shape[0]
    return pl.pallas_call(
        functools.partial(_bias_kernel, win_rows=win_rows),
        grid=(heads,),
        in_specs=[pl.BlockSpec(memory_space=pltpu.SMEM)],
        out_specs=pl.BlockSpec((None, win_rows, GRID_W, win_rows * GRID_W), lambda h: (h, 0, 0, 0)),
        out_shape=jax.ShapeDtypeStruct((heads, win_rows, GRID_W, win_rows * GRID_W), F32),
        compiler_params=_cparams(("arbitrary",), 16 << 20),
        name="na_bias",
    )(rpb.reshape(-1))


def _lat_attn_kernel(q_ref, k_ref, v_ref, ck_ref, cv_ref, qw_ref, kw_ref, bias_ref, o_ref,
                     qn_ref, kn_ref, vb_ref, *, seq_len, win_rows, unroll):
    rows = seq_len // GRID_W
    n_blk = seq_len // GROUP
    band = win_rows * GRID_W
    scale = HEAD_DIM ** -0.5

    def prep(bi, carry):
        r0 = pl.multiple_of(bi * GROUP, GROUP)
        qn_ref[pl.ds(r0, GROUP), :] = _head_rms(q_ref[pl.ds(r0, GROUP), :], qw_ref[...]).astype(BF16)
        kn_ref[pl.ds(r0, GROUP), :] = _head_rms(k_ref[pl.ds(r0, GROUP), :], kw_ref[...]).astype(BF16)
        vb_ref[pl.ds(r0, GROUP), :] = v_ref[pl.ds(r0, GROUP), :].astype(BF16)
        return carry

    lax.fori_loop(0, n_blk, prep, 0)
    ck = ck_ref[...].astype(BF16)
    cv = cv_ref[...].astype(BF16)

    def body(r, carry):
        rs = jnp.clip(r - win_rows // 2, 0, rows - win_rows)
        q0 = pl.multiple_of(r * GRID_W, GRID_W)
        k0 = pl.multiple_of(rs * GRID_W, GRID_W)
        q = qn_ref[pl.ds(q0, GRID_W), :]
        s_loc = _dot_nt(q, kn_ref[pl.ds(k0, band), :]) * scale + bias_ref[r - rs]
        s_ctx = _dot_nt(q, ck) * scale
        p_loc, p_ctx = _softmax_rows([s_loc, s_ctx])
        o = _dot(p_loc.astype(BF16), vb_ref[pl.ds(k0, band), :]) + _dot(p_ctx.astype(BF16), cv)
        o_ref[pl.ds(q0, GRID_W), :] = o.astype(o_ref.dtype)
        return carry

    lax.fori_loop(0, rows, body, 0, unroll=unroll)


def _lat_attention(p_na, cache_k, cache_v, q_w, k_w, bias, *, row0, n_seq, seq_len, heads, layer):
    assert row0 % seq_len == 0 and seq_len % GROUP == 0
    sb0 = row0 // seq_len
    past = cache_k.shape[2]
    win_rows = bias.shape[1]
    rows = seq_len // GRID_W

    def col(off):
        return pl.BlockSpec((seq_len, HEAD_DIM), lambda s, h: (sb0 + s, off + h))

    cache = pl.BlockSpec((None, None, past, HEAD_DIM), lambda s, h: (s, layer, 0, h))
    wspec = pl.BlockSpec((1, HEAD_DIM), lambda s, h: (0, 0))
    return pl.pallas_call(
        functools.partial(_lat_attn_kernel, seq_len=seq_len, win_rows=win_rows,
                          unroll=2 if rows % 2 == 0 else 1),
        grid=(n_seq, heads),
        in_specs=[col(0), col(heads), col(2 * heads), cache, cache, wspec, wspec,
                  pl.BlockSpec((None, win_rows, GRID_W, win_rows * GRID_W), lambda s, h: (h, 0, 0, 0))],
        out_specs=pl.BlockSpec((seq_len, HEAD_DIM), lambda s, h: (s, h)),
        out_shape=jax.ShapeDtypeStruct((n_seq * seq_len, heads * HEAD_DIM), BF16),
        scratch_shapes=[pltpu.VMEM((seq_len, HEAD_DIM), BF16)] * 3,
        compiler_params=_cparams(("parallel", "arbitrary"), 10 * seq_len * HEAD_DIM * 4 + (16 << 20)),
        name="lat_attention",
    )(p_na, p_na, p_na, cache_k.reshape(cache_k.shape[:3] + (-1,)), cache_v.reshape(cache_v.shape[:3] + (-1,)),
      q_w.reshape(1, HEAD_DIM), k_w.reshape(1, HEAD_DIM), bias)


def _fourier_chan_kernel(x_ref, tab_ref, ac_ref, as_ref, *, groups):
    tab = tab_ref[...]
    for g in range(groups):
        cols = slice(g * HEAD_DIM, (g + 1) * HEAD_DIM)
        a = _dot(x_ref[:, cols].astype(BF16), tab)
        ac_ref[:, cols] = a[:, :HEAD_DIM].astype(BF16)
        as_ref[:, cols] = a[:, HEAD_DIM:].astype(BF16)


def _fourier_chan(p_fn, groups):
    nt, width = p_fn.shape
    idx = jnp.arange(HEAD_DIM, dtype=jnp.int32)
    ang = ((idx[:, None] * idx[None, :]) % HEAD_DIM).astype(F32) * (2.0 * math.pi / HEAD_DIM)
    tab = (jnp.concatenate([jnp.cos(ang), jnp.sin(ang)], axis=1) * (HEAD_DIM ** -0.5)).astype(BF16)
    tm = _tile(nt, 512, 8)
    blk = pl.BlockSpec((tm, width), lambda i: (i, 0))
    return pl.pallas_call(
        functools.partial(_fourier_chan_kernel, groups=groups),
        grid=(nt // tm,),
        in_specs=[blk, pl.BlockSpec((HEAD_DIM, 2 * HEAD_DIM), lambda i: (0, 0))],
        out_specs=[blk, blk],
        out_shape=[jax.ShapeDtypeStruct((nt, width), BF16)] * 2,
        compiler_params=_cparams(("parallel",), 24 * tm * width + (8 << 20)),
        name="fourier_chan",
    )(p_fn, tab)


def _fourier_pos_kernel(ct_ref, st_ref, ac_ref, as_ref, o_ref):
    o_ref[...] = (_dot(ct_ref[...], ac_ref[...]) + _dot(st_ref[...], as_ref[...])).astype(o_ref.dtype)


def _fourier_pos(a_cos, a_sin, *, row0, n_seq, seq_len):
    assert row0 % seq_len == 0
    sb0 = row0 // seq_len
    width = a_cos.shape[1]
    idx = jnp.arange(seq_len, dtype=jnp.int32)
    ang = ((idx[:, None] * idx[None, :]) % seq_len).astype(F32) * (2.0 * math.pi / seq_len)
    ct = (jnp.cos(ang) * (seq_len ** -0.5)).astype(BF16)
    st = (-jnp.sin(ang) * (seq_len ** -0.5)).astype(BF16)
    tr = _tile(seq_len, 512, 8)
    tn = _tile(width, 1024 if seq_len <= 512 else 512, 128)
    n_rt = seq_len // tr
    tab = pl.BlockSpec((tr, seq_len), lambda r, s, j: (r, 0))
    src = pl.BlockSpec((seq_len, tn), lambda r, s, j: (sb0 + s, j))
    return pl.pallas_call(
        _fourier_pos_kernel,
        grid=(n_rt, n_seq, width // tn),
        in_specs=[tab, tab, src, src],
        out_specs=pl.BlockSpec((tr, tn), lambda r, s, j: (s * n_rt + r, j)),
        out_shape=jax.ShapeDtypeStruct((n_seq * seq_len, width), BF16),
        compiler_params=_cparams(("arbitrary", "arbitrary", "arbitrary"),
                                 8 * tr * seq_len + 8 * seq_len * tn + 12 * tr * tn + (8 << 20)),
        name="fourier_pos",
    )(ct, st, a_cos, a_sin)


def _rope_tables(seq_len):
    half = HEAD_DIM // 2
    t = jnp.arange(seq_len)
    inv_freq = 1.0 / (ROPE_BASE ** (jnp.arange(0, half, 2, dtype=F32) / half))
    ang_row = (t // GRID_W).astype(F32)[:, None] * inv_freq
    ang_col = (t % GRID_W).astype(F32)[:, None] * inv_freq
    cos = jnp.concatenate([jnp.cos(ang_row)] * 2 + [jnp.cos(ang_col)] * 2, axis=1)
    sin = jnp.concatenate([-jnp.sin(ang_row), jnp.sin(ang_row), -jnp.sin(ang_col), jnp.sin(ang_col)], axis=1)
    return cos, sin


def _gate_layouts(p_ab, heads):
    nt = p_ab.shape[0]
    per_head = p_ab[:, :4 * heads].reshape(nt // GROUP, GROUP, 4, heads)
    return per_head.transpose(3, 0, 1, 2), per_head.transpose(3, 0, 2, 1)


def kernel(x_prompt, x_sample, cache_na_k, cache_na_v, state_gdn, c, c_ctx, norm1_w, norm2_w, ada_w, ada_b, in_w, conv_w, gdn_a_log, gdn_dt_bias, gdn_norm_w, na_q_norm_w, na_k_norm_w, na_rpb, out_w, ffn_gate_w, ffn_up_w, ffn_down_w):
    batch, seq, d = x_prompt.shape
    dec_batch, dec_seq, _ = x_sample.shape
    depth = in_w.shape[0]
    gh = state_gdn.shape[3]
    nh = cache_na_k.shape[3]
    gw, nw = gh * HEAD_DIM, nh * HEAD_DIM
    fw = in_w.shape[2] - 4 * gw - 4 * gh - 3 * nw
    fg = fw // HEAD_DIM
    n_ctx, n_lat = batch * seq, dec_batch * dec_seq
    off_ab = 4 * gw
    off_na = off_ab + 4 * gh
    off_fn = off_na + 3 * nw
    assert dec_batch + 1 <= MOD_ROWS and 4 * gh <= HEAD_DIM
    win_rows = min(NA_ROWS, dec_seq // GRID_W)

    cond = jnp.concatenate([c_ctx[None, :], c, jnp.zeros((MOD_ROWS - 1 - dec_batch, d), F32)], axis=0)
    mods_all = _adaln(cond, ada_w, ada_b)
    rope_tabs = _rope_tables(dec_seq)

    x = jnp.concatenate([x_prompt.reshape(n_ctx, d), x_sample.reshape(n_lat, d)], axis=0)
    new_k, new_v, new_s = [], [], []
    for l in range(depth):
        mods = mods_all[l].reshape(MOD_ROWS * 6, 1, d)
        w_in = in_w[l]
        w_ab = jnp.pad(w_in[:, off_ab:off_na], ((0, 0), (0, HEAD_DIM - 4 * gh))).astype(BF16)
        w_na = w_in[:, off_na:off_fn].astype(BF16)
        w_fn = w_in[:, off_fn:].astype(BF16)

        h = _modulate(x, norm1_w[l], mods, 0, 1, n_ctx, dec_seq)
        p_gdn = _matmul(h, in_w, F32, "proj_gdn", layer=l, n_cols=off_ab)
        p_ab = _matmul(h, w_ab, F32, "proj_gates")
        p_na = _matmul(h, w_na, F32, "proj_na")
        p_fn = _matmul(h, w_fn, F32, "proj_fn")

        gate_col, gate_row = _gate_layouts(p_ab, gh)
        gdn_common = (p_gdn, gate_col, gate_row, conv_w[l], gdn_a_log[l], gdn_dt_bias[l], gdn_norm_w[l])
        gdn_ctx, s_ctx = _gdn(*gdn_common, row0=0, n_seq=batch, seq_len=seq, heads=gh, emit_state=True)
        gdn_lat = _gdn(*gdn_common, row0=n_ctx, n_seq=dec_batch, seq_len=dec_seq, heads=gh,
                       rope_tabs=rope_tabs, s0=state_gdn, layer=l)

        nk_ctx, na_ctx = _ctx_attention(p_na, na_q_norm_w[l], na_k_norm_w[l], n_seq=batch, seq_len=seq, heads=nh)
        bias = _bias_table(na_rpb[l], win_rows)
        na_lat = _lat_attention(p_na, cache_na_k, cache_na_v, na_q_norm_w[l], na_k_norm_w[l], bias,
                                row0=n_ctx, n_seq=dec_batch, seq_len=dec_seq, heads=nh, layer=l)

        a_cos, a_sin = _fourier_chan(p_fn, fg)
        fn_ctx = _fourier_pos(a_cos, a_sin, row0=0, n_seq=batch, seq_len=seq)
        fn_lat = _fourier_pos(a_cos, a_sin, row0=n_ctx, n_seq=dec_batch, seq_len=dec_seq)

        x = _outproj([gdn_ctx, na_ctx, fn_ctx], [gdn_lat, na_lat, fn_lat], out_w[l].astype(BF16),
                     x, mods, 2, n_ctx, dec_seq)

        h2 = _modulate(x, norm2_w[l], mods, 3, 4, n_ctx, dec_seq)
        act = _swiglu(h2, ffn_gate_w, ffn_up_w, l)
        x = _down(act, ffn_down_w[l].astype(BF16), x, mods, 5, n_ctx, dec_seq)

        new_k.append(nk_ctx.reshape(batch, seq, nh, HEAD_DIM))
        new_v.append(p_na[:n_ctx, 2 * nw:].reshape(batch, seq, nh, HEAD_DIM))
        new_s.append(s_ctx)

    y_prompt = x[:n_ctx].reshape(batch, seq, d)
    y_sample = x[n_ctx:].reshape(dec_batch, dec_seq, d)
    return (y_prompt, y_sample, jnp.stack(new_k, axis=1), jnp.stack(new_v, axis=1), jnp.stack(new_s, axis=1))
```

```python
import functools
import math

import jax
import jax.numpy as jnp
import numpy as np
from jax import lax
from jax.experimental import pallas as pl
from jax.experimental.pallas import tpu as pltpu

F32 = jnp.float32
BF16 = jnp.bfloat16

HEAD_DIM = 128
CONV_K = 5
CHUNK = 64
GROUP = 256
GRID_W = 64
NA_ROWS = 8
NA_COLS = 16
ROPE_BASE = 10000.0
EPS = 1e-6
MOD_ROWS = 8
CONV_PAD = 8
GDN_HEADS_PER_STEP = 2
V7X_VMEM_BYTES = 64 * 1024 * 1024
VMEM_CAP = V7X_VMEM_BYTES - 8 * 1024 * 1024


def _cparams(sem, vmem_bytes):
    limit = int(min(max(vmem_bytes, 16 * 1024 * 1024), VMEM_CAP))
    return pltpu.CompilerParams(dimension_semantics=sem, vmem_limit_bytes=limit)


def _tile(n, pref, align):
    if n <= pref:
        return n
    t = (pref // align) * align
    while t >= align:
        if n % t == 0:
            return t
        t -= align
    return n


def _sigmoid(x):
    return 1.0 / (1.0 + jnp.exp(-x))


def _silu(x):
    return x * _sigmoid(x)


def _softplus(x):
    return jnp.maximum(x, 0.0) + jnp.log(1.0 + jnp.exp(-jnp.abs(x)))


def _dot(a, b):
    return jnp.dot(a, b, preferred_element_type=F32)


def _dot_nt(a, b):
    return lax.dot_general(a, b, (((1,), (1,)), ((), ())), preferred_element_type=F32)


def _adaln_kernel(c_ref, w_ref, b_ref, o_ref):
    s = _silu(c_ref[...]).astype(BF16)
    o_ref[...] = _dot(s, w_ref[...].astype(BF16)) + b_ref[...]


def _adaln(cond, ada_w, ada_b):
    depth, d, n = ada_w.shape
    tn = _tile(n, 512, 128)
    return pl.pallas_call(
        _adaln_kernel,
        grid=(depth, n // tn),
        in_specs=[pl.BlockSpec((MOD_ROWS, d), lambda l, j: (0, 0)),
                  pl.BlockSpec((None, d, tn), lambda l, j: (l, 0, j)),
                  pl.BlockSpec((None, 1, tn), lambda l, j: (l, 0, j))],
        out_specs=pl.BlockSpec((None, MOD_ROWS, tn), lambda l, j: (l, 0, j)),
        out_shape=jax.ShapeDtypeStruct((depth, MOD_ROWS, n), F32),
        compiler_params=_cparams(("parallel", "arbitrary"), 3 * d * tn * 4 + (4 << 20)),
        name="adaln",
    )(cond, ada_w, ada_b.reshape(depth, 1, n))


class _Rows:
    def __init__(self, n_ctx, lat_seq, tm):
        assert n_ctx % tm == 0 and lat_seq % tm == 0
        self.ctx_tiles = n_ctx // tm
        self.seq_tiles = lat_seq // tm

    def mod_index(self, i, which):
        row = jnp.where(i < self.ctx_tiles, 0, 1 + (i - self.ctx_tiles) // self.seq_tiles)
        return row * 6 + which


def _modulate_kernel(x_ref, g_ref, sh_ref, sc_ref, o_ref):
    x = x_ref[...]
    y = x * lax.rsqrt(jnp.mean(x * x, axis=-1, keepdims=True) + EPS) * g_ref[...]
    o_ref[...] = (y * (1.0 + sc_ref[...]) + sh_ref[...]).astype(o_ref.dtype)


def _modulate(x, gain, mods, shift_id, scale_id, n_ctx, lat_seq):
    nt, d = x.shape
    tm = _tile(math.gcd(n_ctx, lat_seq), 256, 8)
    rows = _Rows(n_ctx, lat_seq, tm)
    return pl.pallas_call(
        _modulate_kernel,
        grid=(nt // tm,),
        in_specs=[pl.BlockSpec((tm, d), lambda i: (i, 0)),
                  pl.BlockSpec((1, d), lambda i: (0, 0)),
                  pl.BlockSpec((None, 1, d), lambda i: (rows.mod_index(i, shift_id), 0, 0)),
                  pl.BlockSpec((None, 1, d), lambda i: (rows.mod_index(i, scale_id), 0, 0))],
        out_specs=pl.BlockSpec((tm, d), lambda i: (i, 0)),
        out_shape=jax.ShapeDtypeStruct((nt, d), BF16),
        compiler_params=_cparams(("parallel",), 16 * tm * d + (4 << 20)),
        name="modulate",
    )(x, gain.reshape(1, d), mods, mods)


def _mm_kernel(a_ref, w_ref, o_ref):
    o_ref[...] = _dot(a_ref[...], w_ref[...].astype(BF16)).astype(o_ref.dtype)


def _matmul(a, w, out_dtype, name, layer=None, n_cols=None):
    m, k = a.shape
    n = w.shape[-1] if n_cols is None else n_cols
    tm = _tile(m, 1024, 8)
    tn = _tile(n, 512, 128)
    if layer is None:
        w_spec = pl.BlockSpec((k, tn), lambda i, j: (0, j))
    else:
        w_spec = pl.BlockSpec((None, k, tn), lambda i, j: (layer, 0, j))
    wbytes = jnp.dtype(w.dtype).itemsize
    return pl.pallas_call(
        _mm_kernel,
        grid=(m // tm, n // tn),
        in_specs=[pl.BlockSpec((tm, k), lambda i, j: (i, 0)), w_spec],
        out_specs=pl.BlockSpec((tm, tn), lambda i, j: (i, j)),
        out_shape=jax.ShapeDtypeStruct((m, n), out_dtype),
        compiler_params=_cparams(("parallel", "arbitrary"),
                                 4 * tm * k + (2 * wbytes + 2) * k * tn + 12 * tm * tn + (4 << 20)),
        name=name,
    )(a, w)


def _outproj_kernel(*refs, n_parts, widths, kb, ctx_tiles):
    ctx_refs = refs[:n_parts]
    lat_refs = refs[n_parts:2 * n_parts]
    n_w = sum(widths) // kb
    w_refs = refs[2 * n_parts:2 * n_parts + n_w]
    x_ref, g_ref, o_ref = refs[2 * n_parts + n_w:]

    def run(a_refs):
        acc = None
        blk = 0
        for a_ref, width in zip(a_refs, widths):
            for c in range(width // kb):
                term = _dot(a_ref[:, c * kb:(c + 1) * kb], w_refs[blk][...])
                acc = term if acc is None else acc + term
                blk += 1
        o_ref[...] = x_ref[...] + g_ref[...] * acc

    is_ctx = pl.program_id(0) < ctx_tiles

    @pl.when(is_ctx)
    def _():
        run(ctx_refs)

    @pl.when(jnp.logical_not(is_ctx))
    def _():
        run(lat_refs)


def _outproj(ctx_parts, lat_parts, w_out, x, mods, gate_id, n_ctx, lat_seq):
    nt, d = x.shape
    tm = _tile(math.gcd(n_ctx, lat_seq), 1024, 8)
    tn = _tile(d, 512, 128)
    rows = _Rows(n_ctx, lat_seq, tm)
    widths = tuple(p.shape[1] for p in ctx_parts)
    kb = functools.reduce(math.gcd, widths)
    ktot = sum(widths)
    ct = rows.ctx_tiles
    lt = (nt - n_ctx) // tm
    once = pl.Buffered(1)
    a_specs = [pl.BlockSpec((tm, w), lambda i, j: (jnp.minimum(i, ct - 1), 0), pipeline_mode=once)
               for w in widths]
    a_specs += [pl.BlockSpec((tm, w), lambda i, j: (jnp.clip(i - ct, 0, lt - 1), 0), pipeline_mode=once)
                for w in widths]
    w_specs = [pl.BlockSpec((kb, tn), lambda i, j, b=b: (b, j)) for b in range(ktot // kb)]
    return pl.pallas_call(
        functools.partial(_outproj_kernel, n_parts=len(widths), widths=widths, kb=kb, ctx_tiles=ct),
        grid=(nt // tm, d // tn),
        in_specs=a_specs + w_specs + [
            pl.BlockSpec((tm, tn), lambda i, j: (i, j)),
            pl.BlockSpec((None, 1, tn), lambda i, j: (rows.mod_index(i, gate_id), 0, j))],
        out_specs=pl.BlockSpec((tm, tn), lambda i, j: (i, j)),
        out_shape=jax.ShapeDtypeStruct((nt, d), F32),
        compiler_params=_cparams(("parallel", "arbitrary"),
                                 4 * tm * ktot + 4 * ktot * tn + 24 * tm * tn + (4 << 20)),
        name="outproj",
    )(*ctx_parts, *lat_parts, *([w_out] * (ktot // kb)), x, mods)


def _swiglu_kernel(h_ref, wg_ref, wu_ref, o_ref):
    h = h_ref[...]
    g = _dot(h, wg_ref[...].astype(BF16))
    u = _dot(h, wu_ref[...].astype(BF16))
    o_ref[...] = (_silu(g) * u).astype(o_ref.dtype)


def _swiglu(h, wg, wu, layer):
    m, k = h.shape
    n = wg.shape[2]
    tm = _tile(m, 2048, 8)
    tn = _tile(n, 256, 128)
    w_spec = pl.BlockSpec((None, k, tn), lambda i, j: (layer, 0, j))
    return pl.pallas_call(
        _swiglu_kernel,
        grid=(m // tm, n // tn),
        in_specs=[pl.BlockSpec((tm, k), lambda i, j: (i, 0), pipeline_mode=pl.Buffered(1)), w_spec, w_spec],
        out_specs=pl.BlockSpec((tm, tn), lambda i, j: (i, j)),
        out_shape=jax.ShapeDtypeStruct((m, n), BF16),
        compiler_params=_cparams(("parallel", "arbitrary"),
                                 2 * tm * k + 20 * k * tn + 24 * tm * tn + (4 << 20)),
        name="swiglu",
    )(h, wg, wu)


def _down_kernel(a_ref, w_ref, x_ref, g_ref, o_ref, acc_ref, *, k_steps):
    kk = pl.program_id(2)
    part = _dot(a_ref[...], w_ref[...])

    @pl.when(kk == 0)
    def _():
        acc_ref[...] = part

    @pl.when(kk > 0)
    def _():
        acc_ref[...] += part

    @pl.when(kk == k_steps - 1)
    def _():
        o_ref[...] = x_ref[...] + g_ref[...] * acc_ref[...]


def _down(a, w, x, mods, gate_id, n_ctx, lat_seq):
    nt, k = a.shape
    d = w.shape[1]
    tm = _tile(math.gcd(n_ctx, lat_seq), 1024, 8)
    tn = _tile(d, 512, 128)
    tk = _tile(k, 6144, 128)
    rows = _Rows(n_ctx, lat_seq, tm)
    return pl.pallas_call(
        functools.partial(_down_kernel, k_steps=k // tk),
        grid=(nt // tm, d // tn, k // tk),
        in_specs=[pl.BlockSpec((tm, tk), lambda i, j, s: (i, s)),
                  pl.BlockSpec((tk, tn), lambda i, j, s: (s, j)),
                  pl.BlockSpec((tm, tn), lambda i, j, s: (i, j)),
                  pl.BlockSpec((None, 1, tn), lambda i, j, s: (rows.mod_index(i, gate_id), 0, j))],
        out_specs=pl.BlockSpec((tm, tn), lambda i, j, s: (i, j)),
        out_shape=jax.ShapeDtypeStruct((nt, d), F32),
        scratch_shapes=[pltpu.VMEM((tm, tn), F32)],
        compiler_params=_cparams(("parallel", "arbitrary", "arbitrary"),
                                 4 * tm * tk + 4 * tk * tn + 28 * tm * tn + (4 << 20)),
        name="down",
    )(a, w, x, mods)


def _neumann_solve(strict, rhs):
    def split2(v):
        hi = v.astype(BF16)
        return hi, (v - hi.astype(F32)).astype(BF16)

    def dot3(a, b):
        return _dot(a[0], b[0]) + _dot(a[0], b[1]) + _dot(a[1], b[0])

    p = -strict
    x = rhs
    n_fac = int(math.log2(CHUNK))
    for f in range(n_fac):
        ps = split2(p)
        x = x + dot3(ps, split2(x))
        if f < n_fac - 1:
            p = dot3(ps, ps)
    return x


def _gdn_kernel(*refs, seq_len, hps, rotary, has_s0, emit_state):
    it = iter(refs)
    alog_ref, dtb_ref = next(it), next(it)
    q_ref, k_ref, v_ref, z_ref = next(it), next(it), next(it), next(it)
    cw_refs = (next(it), next(it), next(it))
    acol_ref, arow_ref, gw_ref = next(it), next(it), next(it)
    cos_ref = sin_ref = s0_ref = sf_ref = None
    if rotary:
        cos_ref, sin_ref = next(it), next(it)
    if has_s0:
        s0_ref = next(it)
    o_ref = next(it)
    if emit_state:
        sf_ref = next(it)
    pad_refs = (next(it), next(it), next(it))
    qs_ref, kn_ref, vc_ref, oacc_ref, s_ref = next(it), next(it), next(it), next(it), next(it)

    head0 = pl.program_id(1) * hps
    n_groups = seq_len // GROUP
    n_chunks = GROUP // CHUNK
    scale = HEAD_DIM ** -0.5
    width = hps * HEAD_DIM

    def cols(j):
        return slice(j * HEAD_DIM, (j + 1) * HEAD_DIM)

    for src, pad in zip((q_ref, k_ref, v_ref), pad_refs):
        pad[pl.ds(0, CONV_PAD), :] = jnp.zeros((CONV_PAD, width), F32)
        pad[pl.ds(CONV_PAD + seq_len, CONV_PAD), :] = jnp.zeros((CONV_PAD, width), F32)
        pad[pl.ds(CONV_PAD, seq_len), :] = src[...]

    lane = lax.broadcasted_iota(jnp.int32, (GROUP, HEAD_DIM), 1)
    first_half = (lane % 64) < 32

    def conv_silu(pad, cw_ref, g0, j):
        win = pad[pl.ds(g0, GROUP + 2 * CONV_PAD), cols(j)]
        cw = cw_ref[:, cols(j)]
        acc = None
        for tap in range(CONV_K):
            off = CONV_PAD - CONV_K // 2 + tap
            term = win[off:off + GROUP, :] * cw[tap:tap + 1, :]
            acc = term if acc is None else acc + term
        return _silu(acc)

    def l2n(x):
        return x * lax.rsqrt(jnp.sum(x * x, axis=-1, keepdims=True) + EPS)

    def rope(x, g0):
        cos = cos_ref[pl.ds(g0, GROUP), :]
        sin = sin_ref[pl.ds(g0, GROUP), :]
        swapped = jnp.where(first_half, pltpu.roll(x, HEAD_DIM - 32, axis=1), pltpu.roll(x, 32, axis=1))
        return x * cos + swapped * sin

    def prep(gi):
        g0 = pl.multiple_of(gi * GROUP, GROUP)
        for j in range(hps):
            q = l2n(conv_silu(pad_refs[0], cw_refs[0], g0, j))
            k = l2n(conv_silu(pad_refs[1], cw_refs[1], g0, j))
            v = conv_silu(pad_refs[2], cw_refs[2], g0, j)
            if rotary:
                q, k = rope(q, g0), rope(k, g0)
            qs_ref[pl.ds(g0, GROUP), cols(j)] = q * scale
            kn_ref[pl.ds(g0, GROUP), cols(j)] = k
            vc_ref[pl.ds(g0, GROUP), cols(j)] = v

    ri = lax.broadcasted_iota(jnp.int32, (GROUP, GROUP), 0)
    ci = lax.broadcasted_iota(jnp.int32, (GROUP, GROUP), 1)
    same = (ri // CHUNK) == (ci // CHUNK)
    tok_chunk = lax.broadcasted_iota(jnp.int32, (HEAD_DIM, GROUP), 1) // CHUNK
    sub8 = lax.broadcasted_iota(jnp.int32, (8, GROUP), 0)
    lane_chunk = lax.broadcasted_iota(jnp.int32, (1, GROUP), 1) // CHUNK

    def gate_sums(gi, j):
        ac = acol_ref[j, gi]
        ar = arow_ref[j, gi]
        g_c, g_r, beta = [], [], []
        for d in (0, 1):
            alog = alog_ref[d, head0 + j]
            dtb = dtb_ref[d, head0 + j]
            g_c.append(-jnp.exp(jnp.full((GROUP, 1), alog, F32)) * _softplus(ac[:, d:d + 1] + dtb))
            g_r.append(-jnp.exp(jnp.full((1, GROUP), alog, F32)) * _softplus(ar[d:d + 1, :] + dtb))
            beta.append(_sigmoid(ac[:, 2 + d:3 + d]))
        ones_low = jnp.where(same & (ci <= ri), 1.0, 0.0).astype(BF16)
        rem_c = jnp.where(lane == 0, g_c[0], jnp.where(lane == 1, g_c[1], 0.0))
        rem_r = jnp.where(sub8 == 0, g_r[0], jnp.where(sub8 == 1, g_r[1], 0.0))
        pre_c = jnp.zeros((GROUP, HEAD_DIM), F32)
        pre_r = jnp.zeros((8, GROUP), F32)
        for _ in range(3):
            piece_c = rem_c.astype(BF16)
            piece_r = rem_r.astype(BF16)
            pre_c = pre_c + _dot(ones_low, piece_c)
            pre_r = pre_r + _dot_nt(piece_r, ones_low)
            rem_c = rem_c - piece_c.astype(F32)
            rem_r = rem_r - piece_r.astype(F32)
        tot_c = jnp.concatenate(
            [jnp.broadcast_to(pre_c[c * CHUNK + CHUNK - 1:(c + 1) * CHUNK, 1:2], (CHUNK, 1))
             for c in range(n_chunks)], axis=0)
        tot_r = jnp.zeros((1, GROUP), F32)
        for c in range(n_chunks):
            tot_r = jnp.where(lane_chunk == c, pre_r[1:2, c * CHUNK + CHUNK - 1:(c + 1) * CHUNK], tot_r)
        cum_c = (pre_c[:, 0:1], tot_c - pre_c[:, 1:2] + g_c[1])
        cum_r = (pre_r[0:1, :], tot_r - pre_r[1:2, :] + g_r[1])
        return beta, cum_c, cum_r

    def solve_group(gi, direction, j, sums):
        fwd = direction == 0
        g0 = pl.multiple_of(gi * GROUP, GROUP)
        qs = qs_ref[pl.ds(g0, GROUP), cols(j)]
        k = kn_ref[pl.ds(g0, GROUP), cols(j)]
        v = vc_ref[pl.ds(g0, GROUP), cols(j)]
        beta = sums[0][direction]
        gc = sums[1][direction]
        gr = sums[2][direction]
        incl = same & ((ci <= ri) if fwd else (ci >= ri))
        strict = same & ((ci < ri) if fwd else (ci > ri))
        decay = jnp.exp(jnp.where(incl, gc - gr, -jnp.inf))
        kb = k * beta
        both = _dot_nt(jnp.concatenate([kb, qs], axis=0).astype(BF16), k.astype(BF16))
        tri = jnp.where(strict, both[:GROUP] * decay, 0.0)
        intra = (both[GROUP:] * decay).astype(BF16)
        eg = jnp.exp(gc)
        sol = _neumann_solve(tri, jnp.concatenate([v * beta, kb * eg], axis=1))
        k_dec_parts = []
        chunk_decay = []
        for c in range(n_chunks):
            edge = c * CHUNK + (CHUNK - 1 if fwd else 0)
            tot = gc[edge:edge + 1, :]
            chunk_decay.append(jnp.exp(tot))
            k_dec_parts.append(k[c * CHUNK:(c + 1) * CHUNK] * jnp.exp(tot - gc[c * CHUNK:(c + 1) * CHUNK]))
        k_dec_t = jnp.concatenate(k_dec_parts, axis=0).T
        sol_b = sol.astype(BF16)
        lhs = [jnp.where(tok_chunk == c, k_dec_t, 0.0).astype(BF16) for c in range(n_chunks)]
        res = _dot(jnp.concatenate(lhs + [intra], axis=0), sol_b)
        kt_uw = [res[c * HEAD_DIM:(c + 1) * HEAD_DIM] for c in range(n_chunks)]
        a_uw = res[n_chunks * HEAD_DIM:]
        q_eff = (qs * eg - a_uw[:, HEAD_DIM:]).astype(BF16)
        return q_eff, a_uw[:, :HEAD_DIM], kt_uw, chunk_decay

    def run_chain(direction, solved, state):
        q_eff, o_loc, kt_uw, chunk_decay = solved
        outs = [None] * n_chunks
        for c in (range(n_chunks) if direction == 0 else reversed(range(n_chunks))):
            r0, r1 = c * CHUNK, (c + 1) * CHUNK
            lhs = jnp.concatenate([q_eff[r0:r1], kt_uw[c][:, HEAD_DIM:].astype(BF16)], axis=0)
            res = _dot(lhs, state.astype(BF16))
            outs[c] = res[:CHUNK] + o_loc[r0:r1]
            state = state * chunk_decay[c] + kt_uw[c][:, :HEAD_DIM] - res[CHUNK:]
        return jnp.concatenate(outs, axis=0), state

    def finish(o, g0, j):
        y = o * lax.rsqrt(jnp.mean(o * o, axis=-1, keepdims=True) + EPS) * gw_ref[...]
        o_ref[pl.ds(g0, GROUP), cols(j)] = (y * _silu(z_ref[pl.ds(g0, GROUP), cols(j)])).astype(o_ref.dtype)

    def init_state(direction, j):
        if has_s0:
            return s0_ref[direction, j]
        return jnp.zeros((HEAD_DIM, HEAD_DIM), F32)

    if n_groups == 1:
        prep(0)
        for j in range(hps):
            sums = gate_sums(0, j)
            o_f, s_f = run_chain(0, solve_group(0, 0, j, sums), init_state(0, j))
            o_b, s_b = run_chain(1, solve_group(0, 1, j, sums), init_state(1, j))
            if emit_state:
                sf_ref[0, j] = s_f
                sf_ref[1, j] = s_b
            finish(o_f + o_b, 0, j)
        return

    def prep_body(gi, carry):
        prep(gi)
        oacc_ref[pl.ds(pl.multiple_of(gi * GROUP, GROUP), GROUP), :] = jnp.zeros((GROUP, width), F32)
        return carry

    lax.fori_loop(0, n_groups, prep_body, 0)
    for j in range(hps):
        s_ref[0, j] = init_state(0, j)
        s_ref[1, j] = init_state(1, j)

    def scan_body(t, carry):
        gf, gb = t, n_groups - 1 - t
        rf = pl.ds(pl.multiple_of(gf * GROUP, GROUP), GROUP)
        rb = pl.ds(pl.multiple_of(gb * GROUP, GROUP), GROUP)
        for j in range(hps):
            o_f, s_f = run_chain(0, solve_group(gf, 0, j, gate_sums(gf, j)), s_ref[0, j])
            o_b, s_b = run_chain(1, solve_group(gb, 1, j, gate_sums(gb, j)), s_ref[1, j])
            s_ref[0, j] = s_f
            s_ref[1, j] = s_b
            oacc_ref[rf, cols(j)] += o_f
            oacc_ref[rb, cols(j)] += o_b
        return carry

    lax.fori_loop(0, n_groups, scan_body, 0)
    if emit_state:
        sf_ref[...] = s_ref[...]

    def finish_body(gi, carry):
        g0 = pl.multiple_of(gi * GROUP, GROUP)
        for j in range(hps):
            finish(oacc_ref[pl.ds(g0, GROUP), cols(j)], g0, j)
        return carry

    lax.fori_loop(0, n_groups, finish_body, 0)


def _gdn(p_gdn, gate_col, gate_row, conv_w, a_log, dt_bias, norm_w, *, row0, n_seq, seq_len,
         heads, rope_tabs=None, s0=None, layer=None, emit_state=False):
    assert row0 % seq_len == 0 and seq_len % GROUP == 0
    hps = _tile(heads, GDN_HEADS_PER_STEP, 1)
    hb = heads // hps
    width = hps * HEAD_DIM
    sb0 = row0 // seq_len
    gps = seq_len // GROUP
    rotary = rope_tabs is not None
    smem = pl.BlockSpec(memory_space=pltpu.SMEM)

    def col(off):
        return pl.BlockSpec((seq_len, width), lambda s, h: (sb0 + s, off + h))

    def cw(off):
        return pl.BlockSpec((CONV_K, width), lambda s, h: (0, off + h))

    in_specs = [smem, smem, col(0), col(hb), col(2 * hb), col(3 * hb),
                cw(0), cw(hb), cw(2 * hb),
                pl.BlockSpec((hps, gps, GROUP, 4), lambda s, h: (h, sb0 + s, 0, 0)),
                pl.BlockSpec((hps, gps, 4, GROUP), lambda s, h: (h, sb0 + s, 0, 0)),
                pl.BlockSpec((1, HEAD_DIM), lambda s, h: (0, 0))]
    args = [a_log, dt_bias, p_gdn, p_gdn, p_gdn, p_gdn, conv_w, conv_w, conv_w,
            gate_col, gate_row, norm_w.reshape(1, HEAD_DIM)]
    if rotary:
        in_specs += [pl.BlockSpec((seq_len, HEAD_DIM), lambda s, h: (0, 0))] * 2
        args += list(rope_tabs)
    if s0 is not None:
        in_specs.append(pl.BlockSpec((None, None, 2, hps, HEAD_DIM, HEAD_DIM),
                                     lambda s, h: (s, layer, 0, h, 0, 0)))
        args.append(s0)
    out_shape = [jax.ShapeDtypeStruct((n_seq * seq_len, heads * HEAD_DIM), BF16)]
    out_specs = [pl.BlockSpec((seq_len, width), lambda s, h: (s, h))]
    if emit_state:
        out_shape.append(jax.ShapeDtypeStruct((n_seq, 2, heads, HEAD_DIM, HEAD_DIM), F32))
        out_specs.append(pl.BlockSpec((None, 2, hps, HEAD_DIM, HEAD_DIM), lambda s, h: (s, 0, h, 0, 0)))
    scratch = [pltpu.VMEM((seq_len + 2 * CONV_PAD, width), F32)] * 3
    scratch += [pltpu.VMEM((seq_len, width), F32)] * 4
    scratch += [pltpu.VMEM((2, hps, HEAD_DIM, HEAD_DIM), F32)]
    res = pl.pallas_call(
        functools.partial(_gdn_kernel, seq_len=seq_len, hps=hps, rotary=rotary, has_s0=s0 is not None,
                          emit_state=emit_state),
        grid=(n_seq, hb),
        in_specs=in_specs, out_specs=out_specs, out_shape=out_shape, scratch_shapes=scratch,
        compiler_params=_cparams(("parallel", "arbitrary"), 24 * seq_len * width * 4 + (16 << 20)),
        name="gdn_latent" if rotary else "gdn_context",
    )(*args)
    return res if emit_state else res[0]


def _head_rms(x, w):
    return x * lax.rsqrt(jnp.mean(x * x, axis=-1, keepdims=True) + EPS) * w


def _softmax_rows(parts):
    m = None
    for s in parts:
        mi = jnp.max(s, axis=-1, keepdims=True)
        m = mi if m is None else jnp.maximum(m, mi)
    ex = [jnp.exp(s - m) for s in parts]
    den = None
    for e in ex:
        si = jnp.sum(e, axis=-1, keepdims=True)
        den = si if den is None else den + si
    inv = 1.0 / den
    return [e * inv for e in ex]


def _ctx_attn_kernel(q_ref, k_ref, v_ref, qw_ref, kw_ref, nk_ref, o_ref, *, heads_per_step):
    for j in range(heads_per_step):
        cols = slice(j * HEAD_DIM, (j + 1) * HEAD_DIM)
        q = _head_rms(q_ref[:, cols], qw_ref[...])
        k = _head_rms(k_ref[:, cols], kw_ref[...])
        nk_ref[:, cols] = k
        s = _dot_nt(q.astype(BF16), k.astype(BF16)) * (HEAD_DIM ** -0.5)
        (p,) = _softmax_rows([s])
        o_ref[:, cols] = _dot(p.astype(BF16), v_ref[:, cols].astype(BF16)).astype(o_ref.dtype)


def _ctx_attention(p_na, q_w, k_w, *, n_seq, seq_len, heads):
    hps = _tile(heads, 8, 1)
    hb = heads // hps
    width = hps * HEAD_DIM

    def col(off):
        return pl.BlockSpec((seq_len, width), lambda s, h: (s, off + h))

    wspec = pl.BlockSpec((1, HEAD_DIM), lambda s, h: (0, 0))
    rows = n_seq * seq_len
    return pl.pallas_call(
        functools.partial(_ctx_attn_kernel, heads_per_step=hps),
        grid=(n_seq, hb),
        in_specs=[col(0), col(hb), col(2 * hb), wspec, wspec],
        out_specs=[col(0), col(0)],
        out_shape=[jax.ShapeDtypeStruct((rows, heads * HEAD_DIM), F32),
                   jax.ShapeDtypeStruct((rows, heads * HEAD_DIM), BF16)],
        compiler_params=_cparams(("parallel", "arbitrary"), 40 * seq_len * width + (8 << 20)),
        name="ctx_attention",
    )(p_na, p_na, p_na, q_w.reshape(1, HEAD_DIM), k_w.reshape(1, HEAD_DIM))


def _bias_kernel(rpb_ref, o_ref, *, win_rows):
    head = pl.program_id(0)
    n_dr = 2 * NA_ROWS - 1
    n_dc = 2 * NA_COLS - 1
    qc = lax.broadcasted_iota(jnp.int32, (GRID_W, GRID_W), 0)
    kc = lax.broadcasted_iota(jnp.int32, (GRID_W, GRID_W), 1)
    dc = jnp.clip(kc - qc, -(NA_COLS - 1), NA_COLS - 1) + (NA_COLS - 1)
    start = jnp.clip(qc - NA_COLS // 2, 0, GRID_W - NA_COLS)
    in_win = (kc >= start) & (kc < start + NA_COLS)
    blocks = {}
    for e in range(win_rows):
        for j in range(win_rows):
            dr = j - e + (NA_ROWS - 1)
            if dr in blocks:
                continue
            acc = jnp.zeros((GRID_W, GRID_W), F32)
            for d in range(n_dc):
                acc = jnp.where(dc == d, rpb_ref[(head * n_dr + dr) * n_dc + d], acc)
            blocks[dr] = jnp.where(in_win, acc, -jnp.inf)
    for e in range(win_rows):
        o_ref[e] = jnp.concatenate([blocks[j - e + (NA_ROWS - 1)] for j in range(win_rows)], axis=1)


def _bias_table(rpb, win_rows):
    heads = rpb.shape[0]
    return pl.pallas_call(
        functools.partial(_bias_kernel, win_rows=win_rows),
        grid=(heads,),
        in_specs=[pl.BlockSpec(memory_space=pltpu.SMEM)],
        out_specs=pl.BlockSpec((None, win_rows, GRID_W, win_rows * GRID_W), lambda h: (h, 0, 0, 0)),
        out_shape=jax.ShapeDtypeStruct((heads, win_rows, GRID_W, win_rows * GRID_W), F32),
        compiler_params=_cparams(("arbitrary",), 16 << 20),
        name="na_bias",
    )(rpb.reshape(-1))


def _lat_attn_kernel(q_ref, k_ref, v_ref, ck_ref, cv_ref, qw_ref, kw_ref, bias_ref, o_ref,
                     qn_ref, kn_ref, vb_ref, octx_ref, mctx_ref, lctx_ref, *, seq_len, win_rows, unroll):
    rows = seq_len // GRID_W
    n_blk = seq_len // GROUP
    band = win_rows * GRID_W
    scale = HEAD_DIM ** -0.5
    ck = ck_ref[...].astype(BF16)
    cv = cv_ref[...].astype(BF16)

    def prep(bi, carry):
        r0 = pl.multiple_of(bi * GROUP, GROUP)
        qn = _head_rms(q_ref[pl.ds(r0, GROUP), :], qw_ref[...]).astype(BF16)
        qn_ref[pl.ds(r0, GROUP), :] = qn
        kn_ref[pl.ds(r0, GROUP), :] = _head_rms(k_ref[pl.ds(r0, GROUP), :], kw_ref[...]).astype(BF16)
        vb_ref[pl.ds(r0, GROUP), :] = v_ref[pl.ds(r0, GROUP), :].astype(BF16)
        s_ctx = _dot_nt(qn, ck) * scale
        m_ctx = jnp.max(s_ctx, axis=-1, keepdims=True)
        e_ctx = jnp.exp(s_ctx - m_ctx)
        mctx_ref[pl.ds(r0, GROUP), :] = m_ctx
        lctx_ref[pl.ds(r0, GROUP), :] = jnp.sum(e_ctx, axis=-1, keepdims=True)
        octx_ref[pl.ds(r0, GROUP), :] = _dot(e_ctx.astype(BF16), cv)
        return carry

    lax.fori_loop(0, n_blk, prep, 0)

    def body(r, carry):
        rs = jnp.clip(r - win_rows // 2, 0, rows - win_rows)
        q0 = pl.multiple_of(r * GRID_W, GRID_W)
        k0 = pl.multiple_of(rs * GRID_W, GRID_W)
        q = qn_ref[pl.ds(q0, GRID_W), :]
        s_loc = _dot_nt(q, kn_ref[pl.ds(k0, band), :]) * scale + bias_ref[r - rs]
        m_ctx = mctx_ref[pl.ds(q0, GRID_W), :]
        m = jnp.maximum(jnp.max(s_loc, axis=-1, keepdims=True), m_ctx)
        e_loc = jnp.exp(s_loc - m)
        w_ctx = jnp.exp(m_ctx - m)
        den = jnp.sum(e_loc, axis=-1, keepdims=True) + w_ctx * lctx_ref[pl.ds(q0, GRID_W), :]
        num = _dot(e_loc.astype(BF16), vb_ref[pl.ds(k0, band), :]) + w_ctx * octx_ref[pl.ds(q0, GRID_W), :]
        o_ref[pl.ds(q0, GRID_W), :] = (num * (1.0 / den)).astype(o_ref.dtype)
        return carry

    lax.fori_loop(0, rows, body, 0, unroll=unroll)


def _lat_attention(p_na, cache_k, cache_v, q_w, k_w, bias, *, row0, n_seq, seq_len, heads, layer):
    assert row0 % seq_len == 0 and seq_len % GROUP == 0
    sb0 = row0 // seq_len
    past = cache_k.shape[2]
    win_rows = bias.shape[1]
    rows = seq_len // GRID_W

    def col(off):
        return pl.BlockSpec((seq_len, HEAD_DIM), lambda s, h: (sb0 + s, off + h))

    cache = pl.BlockSpec((None, None, past, HEAD_DIM), lambda s, h: (s, layer, 0, h))
    wspec = pl.BlockSpec((1, HEAD_DIM), lambda s, h: (0, 0))
    return pl.pallas_call(
        functools.partial(_lat_attn_kernel, seq_len=seq_len, win_rows=win_rows,
                          unroll=4 if rows % 4 == 0 else 1),
        grid=(n_seq, heads),
        in_specs=[col(0), col(heads), col(2 * heads), cache, cache, wspec, wspec,
                  pl.BlockSpec((None, win_rows, GRID_W, win_rows * GRID_W), lambda s, h: (h, 0, 0, 0))],
        out_specs=pl.BlockSpec((seq_len, HEAD_DIM), lambda s, h: (s, h)),
        out_shape=jax.ShapeDtypeStruct((n_seq * seq_len, heads * HEAD_DIM), BF16),
        scratch_shapes=[pltpu.VMEM((seq_len, HEAD_DIM), BF16)] * 3 + [pltpu.VMEM((seq_len, HEAD_DIM), F32)]
        + [pltpu.VMEM((seq_len, 1), F32)] * 2,
        compiler_params=_cparams(("parallel", "arbitrary"), 16 * seq_len * HEAD_DIM * 4 + (16 << 20)),
        name="lat_attention",
    )(p_na, p_na, p_na, cache_k.reshape(cache_k.shape[:3] + (-1,)), cache_v.reshape(cache_v.shape[:3] + (-1,)),
      q_w.reshape(1, HEAD_DIM), k_w.reshape(1, HEAD_DIM), bias)


def _dft_cos_sin(n):
    idx = np.arange(n, dtype=np.int64)
    ang = ((idx[:, None] * idx[None, :]) % n).astype(np.float64) * (2.0 * math.pi / n)
    return (np.cos(ang) / math.sqrt(n)).astype(np.float32), (np.sin(ang) / math.sqrt(n)).astype(np.float32)


def _fourier_chan_kernel(x_ref, tab_ref, ac_ref, as_ref, *, groups):
    tab = tab_ref[...]
    for g in range(groups):
        cols = slice(g * HEAD_DIM, (g + 1) * HEAD_DIM)
        a = _dot(x_ref[:, cols].astype(BF16), tab)
        ac_ref[:, cols] = a[:, :HEAD_DIM].astype(BF16)
        as_ref[:, cols] = a[:, HEAD_DIM:].astype(BF16)


def _fourier_chan(p_fn, groups):
    nt, width = p_fn.shape
    cos, sin = _dft_cos_sin(HEAD_DIM)
    tab = jnp.asarray(np.concatenate([cos, sin], axis=1), dtype=BF16)
    tm = _tile(nt, 512, 8)
    blk = pl.BlockSpec((tm, width), lambda i: (i, 0))
    return pl.pallas_call(
        functools.partial(_fourier_chan_kernel, groups=groups),
        grid=(nt // tm,),
        in_specs=[blk, pl.BlockSpec((HEAD_DIM, 2 * HEAD_DIM), lambda i: (0, 0))],
        out_specs=[blk, blk],
        out_shape=[jax.ShapeDtypeStruct((nt, width), BF16)] * 2,
        compiler_params=_cparams(("parallel",), 24 * tm * width + (8 << 20)),
        name="fourier_chan",
    )(p_fn, tab)


def _fourier_pos_kernel(ct_ref, st_ref, ac_ref, as_ref, o_ref):
    o_ref[...] = (_dot(ct_ref[...], ac_ref[...]) + _dot(st_ref[...], as_ref[...])).astype(o_ref.dtype)


def _fourier_pos(a_cos, a_sin, *, row0, n_seq, seq_len):
    assert row0 % seq_len == 0
    sb0 = row0 // seq_len
    width = a_cos.shape[1]
    cos, sin = _dft_cos_sin(seq_len)
    ct = jnp.asarray(cos, dtype=BF16)
    st = jnp.asarray(-sin, dtype=BF16)
    tr = _tile(seq_len, 512, 8)
    tn = _tile(width, 1024 if seq_len <= 512 else 512, 128)
    n_rt = seq_len // tr
    tab = pl.BlockSpec((tr, seq_len), lambda r, s, j: (r, 0))
    src = pl.BlockSpec((seq_len, tn), lambda r, s, j: (sb0 + s, j))
    return pl.pallas_call(
        _fourier_pos_kernel,
        grid=(n_rt, n_seq, width // tn),
        in_specs=[tab, tab, src, src],
        out_specs=pl.BlockSpec((tr, tn), lambda r, s, j: (s * n_rt + r, j)),
        out_shape=jax.ShapeDtypeStruct((n_seq * seq_len, width), BF16),
        compiler_params=_cparams(("arbitrary", "arbitrary", "arbitrary"),
                                 8 * tr * seq_len + 8 * seq_len * tn + 12 * tr * tn + (8 << 20)),
        name="fourier_pos",
    )(ct, st, a_cos, a_sin)


def _rope_tables(seq_len):
    half = HEAD_DIM // 2
    t = np.arange(seq_len)
    inv_freq = (1.0 / (np.float32(ROPE_BASE) ** (np.arange(0, half, 2, dtype=np.float32) / np.float32(half)))).astype(np.float32)
    ang_row = ((t // GRID_W).astype(np.float32)[:, None] * inv_freq).astype(np.float64)
    ang_col = ((t % GRID_W).astype(np.float32)[:, None] * inv_freq).astype(np.float64)
    cos = np.concatenate([np.cos(ang_row)] * 2 + [np.cos(ang_col)] * 2, axis=1)
    sin = np.concatenate([-np.sin(ang_row), np.sin(ang_row), -np.sin(ang_col), np.sin(ang_col)], axis=1)
    return jnp.asarray(cos, dtype=F32), jnp.asarray(sin, dtype=F32)


def _gate_layouts(p_ab, heads):
    nt = p_ab.shape[0]
    per_head = p_ab[:, :4 * heads].reshape(nt // GROUP, GROUP, 4, heads)
    return per_head.transpose(3, 0, 1, 2), per_head.transpose(3, 0, 2, 1)


def kernel(x_prompt, x_sample, cache_na_k, cache_na_v, state_gdn, c, c_ctx, norm1_w, norm2_w, ada_w, ada_b, in_w, conv_w, gdn_a_log, gdn_dt_bias, gdn_norm_w, na_q_norm_w, na_k_norm_w, na_rpb, out_w, ffn_gate_w, ffn_up_w, ffn_down_w):
    batch, seq, d = x_prompt.shape
    dec_batch, dec_seq, _ = x_sample.shape
    depth = in_w.shape[0]
    gh = state_gdn.shape[3]
    nh = cache_na_k.shape[3]
    gw, nw = gh * HEAD_DIM, nh * HEAD_DIM
    fw = in_w.shape[2] - 4 * gw - 4 * gh - 3 * nw
    fg = fw // HEAD_DIM
    n_ctx, n_lat = batch * seq, dec_batch * dec_seq
    off_ab = 4 * gw
    off_na = off_ab + 4 * gh
    off_fn = off_na + 3 * nw
    assert dec_batch + 1 <= MOD_ROWS and 4 * gh <= HEAD_DIM
    win_rows = min(NA_ROWS, dec_seq // GRID_W)

    cond = jnp.concatenate([c_ctx[None, :], c, jnp.zeros((MOD_ROWS - 1 - dec_batch, d), F32)], axis=0)
    mods_all = _adaln(cond, ada_w, ada_b)
    rope_tabs = _rope_tables(dec_seq)

    x = jnp.concatenate([x_prompt.reshape(n_ctx, d), x_sample.reshape(n_lat, d)], axis=0)
    new_k, new_v, new_s = [], [], []
    for l in range(depth):
        mods = mods_all[l].reshape(MOD_ROWS * 6, 1, d)
        w_in = in_w[l]
        w_ab = jnp.pad(w_in[:, off_ab:off_na], ((0, 0), (0, HEAD_DIM - 4 * gh))).astype(BF16)
        w_na = w_in[:, off_na:off_fn].astype(BF16)
        w_fn = w_in[:, off_fn:].astype(BF16)

        h = _modulate(x, norm1_w[l], mods, 0, 1, n_ctx, dec_seq)
        p_gdn = _matmul(h, w_in[:, :off_ab].astype(BF16), F32, "proj_gdn")
        p_ab = _matmul(h, w_ab, F32, "proj_gates")
        p_na = _matmul(h, w_na, F32, "proj_na")
        p_fn = _matmul(h, w_fn, F32, "proj_fn")

        gate_col, gate_row = _gate_layouts(p_ab, gh)
        gdn_common = (p_gdn, gate_col, gate_row, conv_w[l], gdn_a_log[l], gdn_dt_bias[l], gdn_norm_w[l])
        gdn_ctx, s_ctx = _gdn(*gdn_common, row0=0, n_seq=batch, seq_len=seq, heads=gh, emit_state=True)
        gdn_lat = _gdn(*gdn_common, row0=n_ctx, n_seq=dec_batch, seq_len=dec_seq, heads=gh,
                       rope_tabs=rope_tabs, s0=state_gdn, layer=l)

        nk_ctx, na_ctx = _ctx_attention(p_na, na_q_norm_w[l], na_k_norm_w[l], n_seq=batch, seq_len=seq, heads=nh)
        bias = _bias_table(na_rpb[l], win_rows)
        na_lat = _lat_attention(p_na, cache_na_k, cache_na_v, na_q_norm_w[l], na_k_norm_w[l], bias,
                                row0=n_ctx, n_seq=dec_batch, seq_len=dec_seq, heads=nh, layer=l)

        a_cos, a_sin = _fourier_chan(p_fn, fg)
        fn_ctx = _fourier_pos(a_cos, a_sin, row0=0, n_seq=batch, seq_len=seq)
        fn_lat = _fourier_pos(a_cos, a_sin, row0=n_ctx, n_seq=dec_batch, seq_len=dec_seq)

        x = _outproj([gdn_ctx, na_ctx, fn_ctx], [gdn_lat, na_lat, fn_lat], out_w[l].astype(BF16),
                     x, mods, 2, n_ctx, dec_seq)

        h2 = _modulate(x, norm2_w[l], mods, 3, 4, n_ctx, dec_seq)
        act = _swiglu(h2, ffn_gate_w, ffn_up_w, l)
        x = _down(act, ffn_down_w[l].astype(BF16), x, mods, 5, n_ctx, dec_seq)

        new_k.append(nk_ctx.reshape(batch, seq, nh, HEAD_DIM))
        new_v.append(p_na[:n_ctx, 2 * nw:].reshape(batch, seq, nh, HEAD_DIM))
        new_s.append(s_ctx)

    y_prompt = x[:n_ctx].reshape(batch, seq, d)
    y_sample = x[n_ctx:].reshape(dec_batch, dec_seq, d)
    return (y_prompt, y_sample, jnp.stack(new_k, axis=1), jnp.stack(new_v, axis=1), jnp.stack(new_s, axis=1))
```

```python
import functools
import math

import jax
import jax.numpy as jnp
import numpy as np
from jax import lax
from jax.experimental import pallas as pl
from jax.experimental.pallas import tpu as pltpu

F32 = jnp.float32
BF16 = jnp.bfloat16

HEAD_DIM = 128
CONV_K = 5
CHUNK = 64
GROUP = 256
GRID_W = 64
NA_ROWS = 8
NA_COLS = 16
ROPE_BASE = 10000.0
EPS = 1e-6
MOD_ROWS = 8
CONV_PAD = 8
GDN_HEADS_PER_STEP = 2
V7X_VMEM_BYTES = 64 * 1024 * 1024
VMEM_CAP = V7X_VMEM_BYTES - 8 * 1024 * 1024


def _cparams(sem, vmem_bytes):
    limit = int(min(max(vmem_bytes, 16 * 1024 * 1024), VMEM_CAP))
    return pltpu.CompilerParams(dimension_semantics=sem, vmem_limit_bytes=limit)


def _tile(n, pref, align):
    if n <= pref:
        return n
    t = (pref // align) * align
    while t >= align:
        if n % t == 0:
            return t
        t -= align
    return n


def _sigmoid(x):
    return 1.0 / (1.0 + jnp.exp(-x))


def _silu(x):
    return x * _sigmoid(x)


def _softplus(x):
    return jnp.maximum(x, 0.0) + jnp.log(1.0 + jnp.exp(-jnp.abs(x)))


def _dot(a, b):
    return jnp.dot(a, b, preferred_element_type=F32)


def _dot_nt(a, b):
    return lax.dot_general(a, b, (((1,), (1,)), ((), ())), preferred_element_type=F32)


def _adaln_kernel(c_ref, w_ref, b_ref, o_ref):
    s = _silu(c_ref[...]).astype(BF16)
    o_ref[...] = _dot(s, w_ref[...].astype(BF16)) + b_ref[...]


def _adaln(cond, ada_w, ada_b):
    depth, d, n = ada_w.shape
    tn = _tile(n, 512, 128)
    return pl.pallas_call(
        _adaln_kernel,
        grid=(depth, n // tn),
        in_specs=[pl.BlockSpec((MOD_ROWS, d), lambda l, j: (0, 0)),
                  pl.BlockSpec((None, d, tn), lambda l, j: (l, 0, j)),
                  pl.BlockSpec((None, 1, tn), lambda l, j: (l, 0, j))],
        out_specs=pl.BlockSpec((None, MOD_ROWS, tn), lambda l, j: (l, 0, j)),
        out_shape=jax.ShapeDtypeStruct((depth, MOD_ROWS, n), F32),
        compiler_params=_cparams(("parallel", "arbitrary"), 3 * d * tn * 4 + (4 << 20)),
        name="adaln",
    )(cond, ada_w, ada_b.reshape(depth, 1, n))


class _Rows:
    def __init__(self, n_ctx, lat_seq, tm):
        assert n_ctx % tm == 0 and lat_seq % tm == 0
        self.ctx_tiles = n_ctx // tm
        self.seq_tiles = lat_seq // tm

    def mod_index(self, i, which):
        row = jnp.where(i < self.ctx_tiles, 0, 1 + (i - self.ctx_tiles) // self.seq_tiles)
        return row * 6 + which


def _modulate_kernel(x_ref, g_ref, sh_ref, sc_ref, o_ref):
    x = x_ref[...]
    y = x * lax.rsqrt(jnp.mean(x * x, axis=-1, keepdims=True) + EPS) * g_ref[...]
    o_ref[...] = (y * (1.0 + sc_ref[...]) + sh_ref[...]).astype(o_ref.dtype)


def _modulate(x, gain, mods, shift_id, scale_id, n_ctx, lat_seq):
    nt, d = x.shape
    tm = _tile(math.gcd(n_ctx, lat_seq), 256, 8)
    rows = _Rows(n_ctx, lat_seq, tm)
    return pl.pallas_call(
        _modulate_kernel,
        grid=(nt // tm,),
        in_specs=[pl.BlockSpec((tm, d), lambda i: (i, 0)),
                  pl.BlockSpec((1, d), lambda i: (0, 0)),
                  pl.BlockSpec((None, 1, d), lambda i: (rows.mod_index(i, shift_id), 0, 0)),
                  pl.BlockSpec((None, 1, d), lambda i: (rows.mod_index(i, scale_id), 0, 0))],
        out_specs=pl.BlockSpec((tm, d), lambda i: (i, 0)),
        out_shape=jax.ShapeDtypeStruct((nt, d), BF16),
        compiler_params=_cparams(("parallel",), 16 * tm * d + (4 << 20)),
        name="modulate",
    )(x, gain.reshape(1, d), mods, mods)


def _mm_kernel(a_ref, w_ref, o_ref):
    o_ref[...] = _dot(a_ref[...], w_ref[...].astype(BF16)).astype(o_ref.dtype)


def _matmul(a, w, out_dtype, name, layer=None, n_cols=None):
    m, k = a.shape
    n = w.shape[-1] if n_cols is None else n_cols
    tm = _tile(m, 1024, 8)
    tn = _tile(n, 512, 128)
    if layer is None:
        w_spec = pl.BlockSpec((k, tn), lambda i, j: (0, j))
    else:
        w_spec = pl.BlockSpec((None, k, tn), lambda i, j: (layer, 0, j))
    wbytes = jnp.dtype(w.dtype).itemsize
    return pl.pallas_call(
        _mm_kernel,
        grid=(m // tm, n // tn),
        in_specs=[pl.BlockSpec((tm, k), lambda i, j: (i, 0)), w_spec],
        out_specs=pl.BlockSpec((tm, tn), lambda i, j: (i, j)),
        out_shape=jax.ShapeDtypeStruct((m, n), out_dtype),
        compiler_params=_cparams(("parallel", "arbitrary"),
                                 4 * tm * k + (2 * wbytes + 2) * k * tn + 12 * tm * tn + (4 << 20)),
        name=name,
    )(a, w)


def _outproj_kernel(*refs, n_parts, widths, kb, ctx_tiles):
    ctx_refs = refs[:n_parts]
    lat_refs = refs[n_parts:2 * n_parts]
    n_w = sum(widths) // kb
    w_refs = refs[2 * n_parts:2 * n_parts + n_w]
    x_ref, g_ref, o_ref = refs[2 * n_parts + n_w:]

    def run(a_refs):
        acc = None
        blk = 0
        for a_ref, width in zip(a_refs, widths):
            for c in range(width // kb):
                term = _dot(a_ref[:, c * kb:(c + 1) * kb], w_refs[blk][...])
                acc = term if acc is None else acc + term
                blk += 1
        o_ref[...] = x_ref[...] + g_ref[...] * acc

    is_ctx = pl.program_id(0) < ctx_tiles

    @pl.when(is_ctx)
    def _():
        run(ctx_refs)

    @pl.when(jnp.logical_not(is_ctx))
    def _():
        run(lat_refs)


def _outproj(ctx_parts, lat_parts, w_out, x, mods, gate_id, n_ctx, lat_seq):
    nt, d = x.shape
    tm = _tile(math.gcd(n_ctx, lat_seq), 1024, 8)
    tn = _tile(d, 512, 128)
    rows = _Rows(n_ctx, lat_seq, tm)
    widths = tuple(p.shape[1] for p in ctx_parts)
    kb = functools.reduce(math.gcd, widths)
    ktot = sum(widths)
    ct = rows.ctx_tiles
    lt = (nt - n_ctx) // tm
    once = pl.Buffered(1)
    a_specs = [pl.BlockSpec((tm, w), lambda i, j: (jnp.minimum(i, ct - 1), 0), pipeline_mode=once)
               for w in widths]
    a_specs += [pl.BlockSpec((tm, w), lambda i, j: (jnp.clip(i - ct, 0, lt - 1), 0), pipeline_mode=once)
                for w in widths]
    w_specs = [pl.BlockSpec((kb, tn), lambda i, j, b=b: (b, j)) for b in range(ktot // kb)]
    return pl.pallas_call(
        functools.partial(_outproj_kernel, n_parts=len(widths), widths=widths, kb=kb, ctx_tiles=ct),
        grid=(nt // tm, d // tn),
        in_specs=a_specs + w_specs + [
            pl.BlockSpec((tm, tn), lambda i, j: (i, j)),
            pl.BlockSpec((None, 1, tn), lambda i, j: (rows.mod_index(i, gate_id), 0, j))],
        out_specs=pl.BlockSpec((tm, tn), lambda i, j: (i, j)),
        out_shape=jax.ShapeDtypeStruct((nt, d), F32),
        compiler_params=_cparams(("parallel", "arbitrary"),
                                 4 * tm * ktot + 4 * ktot * tn + 24 * tm * tn + (4 << 20)),
        name="outproj",
    )(*ctx_parts, *lat_parts, *([w_out] * (ktot // kb)), x, mods)


def _swiglu_kernel(h_ref, wg_ref, wu_ref, o_ref):
    h = h_ref[...]
    g = _dot(h, wg_ref[...].astype(BF16))
    u = _dot(h, wu_ref[...].astype(BF16))
    o_ref[...] = (_silu(g) * u).astype(o_ref.dtype)


def _swiglu(h, wg, wu, layer):
    m, k = h.shape
    n = wg.shape[2]
    tm = _tile(m, 2048, 8)
    tn = _tile(n, 256, 128)
    w_spec = pl.BlockSpec((None, k, tn), lambda i, j: (layer, 0, j))
    return pl.pallas_call(
        _swiglu_kernel,
        grid=(m // tm, n // tn),
        in_specs=[pl.BlockSpec((tm, k), lambda i, j: (i, 0), pipeline_mode=pl.Buffered(1)), w_spec, w_spec],
        out_specs=pl.BlockSpec((tm, tn), lambda i, j: (i, j)),
        out_shape=jax.ShapeDtypeStruct((m, n), BF16),
        compiler_params=_cparams(("parallel", "arbitrary"),
                                 2 * tm * k + 20 * k * tn + 24 * tm * tn + (4 << 20)),
        name="swiglu",
    )(h, wg, wu)


def _down_kernel(a_ref, w_ref, x_ref, g_ref, o_ref, acc_ref, *, k_steps):
    kk = pl.program_id(2)
    part = _dot(a_ref[...], w_ref[...])

    @pl.when(kk == 0)
    def _():
        acc_ref[...] = part

    @pl.when(kk > 0)
    def _():
        acc_ref[...] += part

    @pl.when(kk == k_steps - 1)
    def _():
        o_ref[...] = x_ref[...] + g_ref[...] * acc_ref[...]


def _down(a, w, x, mods, gate_id, n_ctx, lat_seq, row0=0, n_rows=None):
    nt, k = a.shape
    d = w.shape[1]
    n_rows = nt if n_rows is None else n_rows
    tm = _tile(math.gcd(n_ctx, lat_seq), 1024, 8)
    tn = _tile(d, 512, 128)
    tk = _tile(k, 6144, 128)
    rows = _Rows(n_ctx, lat_seq, tm)
    assert row0 % tm == 0 and n_rows % tm == 0
    t0 = row0 // tm
    return pl.pallas_call(
        functools.partial(_down_kernel, k_steps=k // tk),
        grid=(n_rows // tm, d // tn, k // tk),
        in_specs=[pl.BlockSpec((tm, tk), lambda i, j, s: (t0 + i, s)),
                  pl.BlockSpec((tk, tn), lambda i, j, s: (s, j)),
                  pl.BlockSpec((tm, tn), lambda i, j, s: (t0 + i, j)),
                  pl.BlockSpec((None, 1, tn), lambda i, j, s: (rows.mod_index(t0 + i, gate_id), 0, j))],
        out_specs=pl.BlockSpec((tm, tn), lambda i, j, s: (i, j)),
        out_shape=jax.ShapeDtypeStruct((n_rows, d), F32),
        scratch_shapes=[pltpu.VMEM((tm, tn), F32)],
        compiler_params=_cparams(("parallel", "arbitrary", "arbitrary"),
                                 4 * tm * tk + 4 * tk * tn + 28 * tm * tn + (4 << 20)),
        name="down",
    )(a, w, x, mods)


def _neumann_solve(strict, rhs):
    def split2(v):
        hi = v.astype(BF16)
        return hi, (v - hi.astype(F32)).astype(BF16)

    def dot3(a, b):
        return _dot(a[0], b[0]) + _dot(a[0], b[1]) + _dot(a[1], b[0])

    p = -strict
    x = rhs
    n_fac = int(math.log2(CHUNK))
    n_fine = n_fac - 2
    for f in range(n_fac):
        ps = split2(p)
        if f < n_fine:
            x = x + dot3(ps, split2(x))
        else:
            x = x + _dot(ps[0], x.astype(BF16))
        if f < n_fac - 1:
            p = dot3(ps, ps) if f < n_fine else _dot(ps[0], ps[0])
    return x


def _gdn_kernel(*refs, seq_len, hps, rotary, has_s0, emit_state):
    it = iter(refs)
    alog_ref, dtb_ref = next(it), next(it)
    q_ref, k_ref, v_ref, z_ref = next(it), next(it), next(it), next(it)
    cw_refs = (next(it), next(it), next(it))
    acol_ref, arow_ref, gw_ref = next(it), next(it), next(it)
    cos_ref = sin_ref = s0_ref = sf_ref = None
    if rotary:
        cos_ref, sin_ref = next(it), next(it)
    if has_s0:
        s0_ref = next(it)
    o_ref = next(it)
    if emit_state:
        sf_ref = next(it)
    pad_refs = (next(it), next(it), next(it))
    qs_ref, kn_ref, vc_ref, oacc_ref, s_ref = next(it), next(it), next(it), next(it), next(it)

    head0 = pl.program_id(1) * hps
    n_groups = seq_len // GROUP
    n_chunks = GROUP // CHUNK
    scale = HEAD_DIM ** -0.5
    width = hps * HEAD_DIM

    def cols(j):
        return slice(j * HEAD_DIM, (j + 1) * HEAD_DIM)

    for src, pad in zip((q_ref, k_ref, v_ref), pad_refs):
        pad[pl.ds(0, CONV_PAD), :] = jnp.zeros((CONV_PAD, width), F32)
        pad[pl.ds(CONV_PAD + seq_len, CONV_PAD), :] = jnp.zeros((CONV_PAD, width), F32)
        pad[pl.ds(CONV_PAD, seq_len), :] = src[...]

    lane = lax.broadcasted_iota(jnp.int32, (GROUP, HEAD_DIM), 1)
    first_half = (lane % 64) < 32

    def conv_silu(pad, cw_ref, g0, j):
        win = pad[pl.ds(g0, GROUP + 2 * CONV_PAD), cols(j)]
        cw = cw_ref[:, cols(j)]
        acc = None
        for tap in range(CONV_K):
            off = CONV_PAD - CONV_K // 2 + tap
            term = win[off:off + GROUP, :] * cw[tap:tap + 1, :]
            acc = term if acc is None else acc + term
        return _silu(acc)

    def l2n(x):
        return x * lax.rsqrt(jnp.sum(x * x, axis=-1, keepdims=True) + EPS)

    def rope(x, g0):
        cos = cos_ref[pl.ds(g0, GROUP), :]
        sin = sin_ref[pl.ds(g0, GROUP), :]
        swapped = jnp.where(first_half, pltpu.roll(x, HEAD_DIM - 32, axis=1), pltpu.roll(x, 32, axis=1))
        return x * cos + swapped * sin

    def prep(gi):
        g0 = pl.multiple_of(gi * GROUP, GROUP)
        for j in range(hps):
            q = l2n(conv_silu(pad_refs[0], cw_refs[0], g0, j))
            k = l2n(conv_silu(pad_refs[1], cw_refs[1], g0, j))
            v = conv_silu(pad_refs[2], cw_refs[2], g0, j)
            if rotary:
                q, k = rope(q, g0), rope(k, g0)
            qs_ref[pl.ds(g0, GROUP), cols(j)] = q * scale
            kn_ref[pl.ds(g0, GROUP), cols(j)] = k
            vc_ref[pl.ds(g0, GROUP), cols(j)] = v

    ri = lax.broadcasted_iota(jnp.int32, (GROUP, GROUP), 0)
    ci = lax.broadcasted_iota(jnp.int32, (GROUP, GROUP), 1)
    same = (ri // CHUNK) == (ci // CHUNK)
    tok_chunk = lax.broadcasted_iota(jnp.int32, (HEAD_DIM, GROUP), 1) // CHUNK
    sub8 = lax.broadcasted_iota(jnp.int32, (8, GROUP), 0)
    lane_chunk = lax.broadcasted_iota(jnp.int32, (1, GROUP), 1) // CHUNK

    def gate_sums(gi, j):
        ac = acol_ref[j, gi]
        ar = arow_ref[j, gi]
        g_c, g_r, beta = [], [], []
        for d in (0, 1):
            alog = alog_ref[d, head0 + j]
            dtb = dtb_ref[d, head0 + j]
            g_c.append(-jnp.exp(jnp.full((GROUP, 1), alog, F32)) * _softplus(ac[:, d:d + 1] + dtb))
            g_r.append(-jnp.exp(jnp.full((1, GROUP), alog, F32)) * _softplus(ar[d:d + 1, :] + dtb))
            beta.append(_sigmoid(ac[:, 2 + d:3 + d]))
        ones_low = jnp.where(same & (ci <= ri), 1.0, 0.0).astype(BF16)
        rem_c = jnp.where(lane == 0, g_c[0], jnp.where(lane == 1, g_c[1], 0.0))
        rem_r = jnp.where(sub8 == 0, g_r[0], jnp.where(sub8 == 1, g_r[1], 0.0))
        pre_c = jnp.zeros((GROUP, HEAD_DIM), F32)
        pre_r = jnp.zeros((8, GROUP), F32)
        for _ in range(3):
            piece_c = rem_c.astype(BF16)
            piece_r = rem_r.astype(BF16)
            pre_c = pre_c + _dot(ones_low, piece_c)
            pre_r = pre_r + _dot_nt(piece_r, ones_low)
            rem_c = rem_c - piece_c.astype(F32)
            rem_r = rem_r - piece_r.astype(F32)
        tot_c = jnp.concatenate(
            [jnp.broadcast_to(pre_c[c * CHUNK + CHUNK - 1:(c + 1) * CHUNK, 1:2], (CHUNK, 1))
             for c in range(n_chunks)], axis=0)
        tot_r = jnp.zeros((1, GROUP), F32)
        for c in range(n_chunks):
            tot_r = jnp.where(lane_chunk == c, pre_r[1:2, c * CHUNK + CHUNK - 1:(c + 1) * CHUNK], tot_r)
        cum_c = (pre_c[:, 0:1], tot_c - pre_c[:, 1:2] + g_c[1])
        cum_r = (pre_r[0:1, :], tot_r - pre_r[1:2, :] + g_r[1])
        return beta, cum_c, cum_r

    def solve_group(gi, direction, j, sums):
        fwd = direction == 0
        g0 = pl.multiple_of(gi * GROUP, GROUP)
        qs = qs_ref[pl.ds(g0, GROUP), cols(j)]
        k = kn_ref[pl.ds(g0, GROUP), cols(j)]
        v = vc_ref[pl.ds(g0, GROUP), cols(j)]
        beta = sums[0][direction]
        gc = sums[1][direction]
        gr = sums[2][direction]
        incl = same & ((ci <= ri) if fwd else (ci >= ri))
        strict = same & ((ci < ri) if fwd else (ci > ri))
        decay = jnp.exp(jnp.where(incl, gc - gr, -jnp.inf))
        kb = k * beta
        both = _dot_nt(jnp.concatenate([kb, qs], axis=0).astype(BF16), k.astype(BF16))
        tri = jnp.where(strict, both[:GROUP] * decay, 0.0)
        intra = (both[GROUP:] * decay).astype(BF16)
        eg = jnp.exp(gc)
        sol = _neumann_solve(tri, jnp.concatenate([v * beta, kb * eg], axis=1))
        k_dec_parts = []
        chunk_decay = []
        for c in range(n_chunks):
            edge = c * CHUNK + (CHUNK - 1 if fwd else 0)
            tot = gc[edge:edge + 1, :]
            chunk_decay.append(jnp.exp(tot))
            k_dec_parts.append(k[c * CHUNK:(c + 1) * CHUNK] * jnp.exp(tot - gc[c * CHUNK:(c + 1) * CHUNK]))
        k_dec_t = jnp.concatenate(k_dec_parts, axis=0).T
        sol_b = sol.astype(BF16)
        lhs = [jnp.where(tok_chunk == c, k_dec_t, 0.0).astype(BF16) for c in range(n_chunks)]
        res = _dot(jnp.concatenate(lhs + [intra], axis=0), sol_b)
        kt_uw = [res[c * HEAD_DIM:(c + 1) * HEAD_DIM] for c in range(n_chunks)]
        a_uw = res[n_chunks * HEAD_DIM:]
        q_eff = (qs * eg - a_uw[:, HEAD_DIM:]).astype(BF16)
        return q_eff, a_uw[:, :HEAD_DIM], kt_uw, chunk_decay

    def run_chain(direction, solved, state):
        q_eff, o_loc, kt_uw, chunk_decay = solved
        outs = [None] * n_chunks
        for c in (range(n_chunks) if direction == 0 else reversed(range(n_chunks))):
            r0, r1 = c * CHUNK, (c + 1) * CHUNK
            lhs = jnp.concatenate([q_eff[r0:r1], kt_uw[c][:, HEAD_DIM:].astype(BF16)], axis=0)
            res = _dot(lhs, state.astype(BF16))
            outs[c] = res[:CHUNK] + o_loc[r0:r1]
            state = state * chunk_decay[c] + kt_uw[c][:, :HEAD_DIM] - res[CHUNK:]
        return jnp.concatenate(outs, axis=0), state

    def finish(o, g0, j):
        y = o * lax.rsqrt(jnp.mean(o * o, axis=-1, keepdims=True) + EPS) * gw_ref[...]
        o_ref[pl.ds(g0, GROUP), cols(j)] = (y * _silu(z_ref[pl.ds(g0, GROUP), cols(j)])).astype(o_ref.dtype)

    def init_state(direction, j):
        if has_s0:
            return s0_ref[direction, j]
        return jnp.zeros((HEAD_DIM, HEAD_DIM), F32)

    if n_groups == 1:
        prep(0)
        for j in range(hps):
            sums = gate_sums(0, j)
            o_f, s_f = run_chain(0, solve_group(0, 0, j, sums), init_state(0, j))
            o_b, s_b = run_chain(1, solve_group(0, 1, j, sums), init_state(1, j))
            if emit_state:
                sf_ref[0, j] = s_f
                sf_ref[1, j] = s_b
            finish(o_f + o_b, 0, j)
        return

    def prep_body(gi, carry):
        prep(gi)
        oacc_ref[pl.ds(pl.multiple_of(gi * GROUP, GROUP), GROUP), :] = jnp.zeros((GROUP, width), F32)
        return carry

    lax.fori_loop(0, n_groups, prep_body, 0)
    for j in range(hps):
        s_ref[0, j] = init_state(0, j)
        s_ref[1, j] = init_state(1, j)

    def scan_body(t, carry):
        gf, gb = t, n_groups - 1 - t
        rf = pl.ds(pl.multiple_of(gf * GROUP, GROUP), GROUP)
        rb = pl.ds(pl.multiple_of(gb * GROUP, GROUP), GROUP)
        for j in range(hps):
            o_f, s_f = run_chain(0, solve_group(gf, 0, j, gate_sums(gf, j)), s_ref[0, j])
            o_b, s_b = run_chain(1, solve_group(gb, 1, j, gate_sums(gb, j)), s_ref[1, j])
            s_ref[0, j] = s_f
            s_ref[1, j] = s_b
            oacc_ref[rf, cols(j)] += o_f
            oacc_ref[rb, cols(j)] += o_b
        return carry

    lax.fori_loop(0, n_groups, scan_body, 0)
    if emit_state:
        sf_ref[...] = s_ref[...]

    def finish_body(gi, carry):
        g0 = pl.multiple_of(gi * GROUP, GROUP)
        for j in range(hps):
            finish(oacc_ref[pl.ds(g0, GROUP), cols(j)], g0, j)
        return carry

    lax.fori_loop(0, n_groups, finish_body, 0)


def _gdn(p_gdn, gate_col, gate_row, conv_w, a_log, dt_bias, norm_w, *, row0, n_seq, seq_len,
         heads, rope_tabs=None, s0=None, layer=None, emit_state=False):
    assert row0 % seq_len == 0 and seq_len % GROUP == 0
    hps = _tile(heads, GDN_HEADS_PER_STEP, 1)
    hb = heads // hps
    width = hps * HEAD_DIM
    sb0 = row0 // seq_len
    gps = seq_len // GROUP
    rotary = rope_tabs is not None
    smem = pl.BlockSpec(memory_space=pltpu.SMEM)

    def col(off):
        return pl.BlockSpec((seq_len, width), lambda s, h: (sb0 + s, off + h))

    def cw(off):
        return pl.BlockSpec((CONV_K, width), lambda s, h: (0, off + h))

    in_specs = [smem, smem, col(0), col(hb), col(2 * hb), col(3 * hb),
                cw(0), cw(hb), cw(2 * hb),
                pl.BlockSpec((hps, gps, GROUP, 4), lambda s, h: (h, sb0 + s, 0, 0)),
                pl.BlockSpec((hps, gps, 4, GROUP), lambda s, h: (h, sb0 + s, 0, 0)),
                pl.BlockSpec((1, HEAD_DIM), lambda s, h: (0, 0))]
    args = [a_log, dt_bias, p_gdn, p_gdn, p_gdn, p_gdn, conv_w, conv_w, conv_w,
            gate_col, gate_row, norm_w.reshape(1, HEAD_DIM)]
    if rotary:
        in_specs += [pl.BlockSpec((seq_len, HEAD_DIM), lambda s, h: (0, 0))] * 2
        args += list(rope_tabs)
    if s0 is not None:
        in_specs.append(pl.BlockSpec((None, None, 2, hps, HEAD_DIM, HEAD_DIM),
                                     lambda s, h: (s, layer, 0, h, 0, 0)))
        args.append(s0)
    out_shape = [jax.ShapeDtypeStruct((n_seq * seq_len, heads * HEAD_DIM), BF16)]
    out_specs = [pl.BlockSpec((seq_len, width), lambda s, h: (s, h))]
    if emit_state:
        out_shape.append(jax.ShapeDtypeStruct((n_seq, 2, heads, HEAD_DIM, HEAD_DIM), F32))
        out_specs.append(pl.BlockSpec((None, 2, hps, HEAD_DIM, HEAD_DIM), lambda s, h: (s, 0, h, 0, 0)))
    scratch = [pltpu.VMEM((seq_len + 2 * CONV_PAD, width), F32)] * 3
    scratch += [pltpu.VMEM((seq_len, width), F32)] * 4
    scratch += [pltpu.VMEM((2, hps, HEAD_DIM, HEAD_DIM), F32)]
    res = pl.pallas_call(
        functools.partial(_gdn_kernel, seq_len=seq_len, hps=hps, rotary=rotary, has_s0=s0 is not None,
                          emit_state=emit_state),
        grid=(n_seq, hb),
        in_specs=in_specs, out_specs=out_specs, out_shape=out_shape, scratch_shapes=scratch,
        compiler_params=_cparams(("parallel", "arbitrary"), 24 * seq_len * width * 4 + (16 << 20)),
        name="gdn_latent" if rotary else "gdn_context",
    )(*args)
    return res if emit_state else res[0]


def _head_rms(x, w):
    return x * lax.rsqrt(jnp.mean(x * x, axis=-1, keepdims=True) + EPS) * w


def _softmax_rows(parts):
    m = None
    for s in parts:
        mi = jnp.max(s, axis=-1, keepdims=True)
        m = mi if m is None else jnp.maximum(m, mi)
    ex = [jnp.exp(s - m) for s in parts]
    den = None
    for e in ex:
        si = jnp.sum(e, axis=-1, keepdims=True)
        den = si if den is None else den + si
    inv = 1.0 / den
    return [e * inv for e in ex]


def _ctx_attn_kernel(q_ref, k_ref, v_ref, qw_ref, kw_ref, nk_ref, nv_ref, o_ref, *, heads_per_step):
    for j in range(heads_per_step):
        cols = slice(j * HEAD_DIM, (j + 1) * HEAD_DIM)
        q = _head_rms(q_ref[:, cols], qw_ref[...])
        k = _head_rms(k_ref[:, cols], kw_ref[...])
        v = v_ref[:, cols]
        nk_ref[:, cols] = k
        nv_ref[:, cols] = v
        s = _dot_nt(q.astype(BF16), k.astype(BF16)) * (HEAD_DIM ** -0.5)
        (p,) = _softmax_rows([s])
        o_ref[:, cols] = _dot(p.astype(BF16), v.astype(BF16)).astype(o_ref.dtype)


def _ctx_attention(p_na, q_w, k_w, *, n_seq, seq_len, heads):
    hps = _tile(heads, 8, 1)
    hb = heads // hps
    width = hps * HEAD_DIM

    def col(off):
        return pl.BlockSpec((seq_len, width), lambda s, h: (s, off + h))

    wspec = pl.BlockSpec((1, HEAD_DIM), lambda s, h: (0, 0))
    rows = n_seq * seq_len
    return pl.pallas_call(
        functools.partial(_ctx_attn_kernel, heads_per_step=hps),
        grid=(n_seq, hb),
        in_specs=[col(0), col(hb), col(2 * hb), wspec, wspec],
        out_specs=[col(0), col(0), col(0)],
        out_shape=[jax.ShapeDtypeStruct((rows, heads * HEAD_DIM), F32),
                   jax.ShapeDtypeStruct((rows, heads * HEAD_DIM), F32),
                   jax.ShapeDtypeStruct((rows, heads * HEAD_DIM), BF16)],
        compiler_params=_cparams(("parallel", "arbitrary"), 48 * seq_len * width + (8 << 20)),
        name="ctx_attention",
    )(p_na, p_na, p_na, q_w.reshape(1, HEAD_DIM), k_w.reshape(1, HEAD_DIM))


def _bias_kernel(rpb_ref, o_ref, *, win_rows):
    head = pl.program_id(0)
    n_dr = 2 * NA_ROWS - 1
    n_dc = 2 * NA_COLS - 1
    qc = lax.broadcasted_iota(jnp.int32, (GRID_W, GRID_W), 0)
    kc = lax.broadcasted_iota(jnp.int32, (GRID_W, GRID_W), 1)
    dc = jnp.clip(kc - qc, -(NA_COLS - 1), NA_COLS - 1) + (NA_COLS - 1)
    start = jnp.clip(qc - NA_COLS // 2, 0, GRID_W - NA_COLS)
    in_win = (kc >= start) & (kc < start + NA_COLS)
    blocks = {}
    for e in range(win_rows):
        for j in range(win_rows):
            dr = j - e + (NA_ROWS - 1)
            if dr in blocks:
                continue
            acc = jnp.zeros((GRID_W, GRID_W), F32)
            for d in range(n_dc):
                acc = jnp.where(dc == d, rpb_ref[(head * n_dr + dr) * n_dc + d], acc)
            blocks[dr] = jnp.where(in_win, acc, -jnp.inf)
    for e in range(win_rows):
        o_ref[e] = jnp.concatenate([blocks[j - e + (NA_ROWS - 1)] for j in range(win_rows)], axis=1)


def _bias_table(rpb, win_rows):
    heads = rpb.shape[0]
    return pl.pallas_call(
        functools.partial(_bias_kernel, win_rows=win_rows),
        grid=(heads,),
        in_specs=[pl.BlockSpec(memory_space=pltpu.SMEM)],
        out_specs=pl.BlockSpec((None, win_rows, GRID_W, win_rows * GRID_W), lambda h: (h, 0, 0, 0)),
        out_shape=jax.ShapeDtypeStruct((heads, win_rows, GRID_W, win_rows * GRID_W), F32),
        compiler_params=_cparams(("arbitrary",), 16 << 20),
        name="na_bias",
    )(rpb.reshape(-1))


def _lat_attn_kernel(q_ref, k_ref, v_ref, ck_ref, cv_ref, qw_ref, kw_ref, bias_ref, o_ref,
                     qn_ref, kn_ref, vb_ref, octx_ref, mctx_ref, lctx_ref, *, seq_len, win_rows, unroll):
    rows = seq_len // GRID_W
    n_blk = seq_len // GROUP
    band = win_rows * GRID_W
    scale = HEAD_DIM ** -0.5
    ck = ck_ref[...].astype(BF16)
    cv = cv_ref[...].astype(BF16)

    def prep(bi, carry):
        r0 = pl.multiple_of(bi * GROUP, GROUP)
        qn = _head_rms(q_ref[pl.ds(r0, GROUP), :], qw_ref[...]).astype(BF16)
        qn_ref[pl.ds(r0, GROUP), :] = qn
        kn_ref[pl.ds(r0, GROUP), :] = _head_rms(k_ref[pl.ds(r0, GROUP), :], kw_ref[...]).astype(BF16)
        vb_ref[pl.ds(r0, GROUP), :] = v_ref[pl.ds(r0, GROUP), :].astype(BF16)
        s_ctx = _dot_nt(qn, ck) * scale
        m_ctx = jnp.max(s_ctx, axis=-1, keepdims=True)
        e_ctx = jnp.exp(s_ctx - m_ctx)
        mctx_ref[pl.ds(r0, GROUP), :] = m_ctx
        lctx_ref[pl.ds(r0, GROUP), :] = jnp.sum(e_ctx, axis=-1, keepdims=True)
        octx_ref[pl.ds(r0, GROUP), :] = _dot(e_ctx.astype(BF16), cv)
        return carry

    lax.fori_loop(0, n_blk, prep, 0)

    def body(r, carry):
        rs = jnp.clip(r - win_rows // 2, 0, rows - win_rows)
        q0 = pl.multiple_of(r * GRID_W, GRID_W)
        k0 = pl.multiple_of(rs * GRID_W, GRID_W)
        q = qn_ref[pl.ds(q0, GRID_W), :]
        s_loc = _dot_nt(q, kn_ref[pl.ds(k0, band), :]) * scale + bias_ref[r - rs]
        m_ctx = mctx_ref[pl.ds(q0, GRID_W), :]
        m = jnp.maximum(jnp.max(s_loc, axis=-1, keepdims=True), m_ctx)
        e_loc = jnp.exp(s_loc - m)
        w_ctx = jnp.exp(m_ctx - m)
        den = jnp.sum(e_loc, axis=-1, keepdims=True) + w_ctx * lctx_ref[pl.ds(q0, GRID_W), :]
        num = _dot(e_loc.astype(BF16), vb_ref[pl.ds(k0, band), :]) + w_ctx * octx_ref[pl.ds(q0, GRID_W), :]
        o_ref[pl.ds(q0, GRID_W), :] = (num * (1.0 / den)).astype(o_ref.dtype)
        return carry

    lax.fori_loop(0, rows, body, 0, unroll=unroll)


def _lat_attention(p_na, cache_k, cache_v, q_w, k_w, bias, *, row0, n_seq, seq_len, heads, layer):
    assert row0 % seq_len == 0 and seq_len % GROUP == 0
    sb0 = row0 // seq_len
    past = cache_k.shape[2]
    win_rows = bias.shape[1]
    rows = seq_len // GRID_W

    def col(off):
        return pl.BlockSpec((seq_len, HEAD_DIM), lambda s, h: (sb0 + s, off + h))

    cache = pl.BlockSpec((None, None, past, HEAD_DIM), lambda s, h: (s, layer, 0, h))
    wspec = pl.BlockSpec((1, HEAD_DIM), lambda s, h: (0, 0))
    return pl.pallas_call(
        functools.partial(_lat_attn_kernel, seq_len=seq_len, win_rows=win_rows,
                          unroll=4 if rows % 4 == 0 else 1),
        grid=(n_seq, heads),
        in_specs=[col(0), col(heads), col(2 * heads), cache, cache, wspec, wspec,
                  pl.BlockSpec((None, win_rows, GRID_W, win_rows * GRID_W), lambda s, h: (h, 0, 0, 0))],
        out_specs=pl.BlockSpec((seq_len, HEAD_DIM), lambda s, h: (s, h)),
        out_shape=jax.ShapeDtypeStruct((n_seq * seq_len, heads * HEAD_DIM), BF16),
        scratch_shapes=[pltpu.VMEM((seq_len, HEAD_DIM), BF16)] * 3 + [pltpu.VMEM((seq_len, HEAD_DIM), F32)]
        + [pltpu.VMEM((seq_len, 1), F32)] * 2,
        compiler_params=_cparams(("parallel", "arbitrary"), 16 * seq_len * HEAD_DIM * 4 + (16 << 20)),
        name="lat_attention",
    )(p_na, p_na, p_na, cache_k.reshape(cache_k.shape[:3] + (-1,)), cache_v.reshape(cache_v.shape[:3] + (-1,)),
      q_w.reshape(1, HEAD_DIM), k_w.reshape(1, HEAD_DIM), bias)


def _dft_cos_sin(n):
    idx = np.arange(n, dtype=np.int64)
    ang = ((idx[:, None] * idx[None, :]) % n).astype(np.float64) * (2.0 * math.pi / n)
    return (np.cos(ang) / math.sqrt(n)).astype(np.float32), (np.sin(ang) / math.sqrt(n)).astype(np.float32)


def _fourier_chan_kernel(x_ref, tab_ref, ac_ref, as_ref, *, groups):
    tab = tab_ref[...]
    for g in range(groups):
        cols = slice(g * HEAD_DIM, (g + 1) * HEAD_DIM)
        a = _dot(x_ref[:, cols].astype(BF16), tab)
        ac_ref[:, cols] = a[:, :HEAD_DIM].astype(BF16)
        as_ref[:, cols] = a[:, HEAD_DIM:].astype(BF16)


def _fourier_chan(p_fn, groups):
    nt, width = p_fn.shape
    cos, sin = _dft_cos_sin(HEAD_DIM)
    tab = jnp.asarray(np.concatenate([cos, sin], axis=1), dtype=BF16)
    tm = _tile(nt, 512, 8)
    blk = pl.BlockSpec((tm, width), lambda i: (i, 0))
    return pl.pallas_call(
        functools.partial(_fourier_chan_kernel, groups=groups),
        grid=(nt // tm,),
        in_specs=[blk, pl.BlockSpec((HEAD_DIM, 2 * HEAD_DIM), lambda i: (0, 0))],
        out_specs=[blk, blk],
        out_shape=[jax.ShapeDtypeStruct((nt, width), BF16)] * 2,
        compiler_params=_cparams(("parallel",), 24 * tm * width + (8 << 20)),
        name="fourier_chan",
    )(p_fn, tab)


def _fourier_pos_kernel(ct_ref, st_ref, ac_ref, as_ref, o_ref):
    o_ref[...] = (_dot(ct_ref[...], ac_ref[...]) + _dot(st_ref[...], as_ref[...])).astype(o_ref.dtype)


def _fourier_pos(a_cos, a_sin, *, row0, n_seq, seq_len):
    assert row0 % seq_len == 0
    sb0 = row0 // seq_len
    width = a_cos.shape[1]
    cos, sin = _dft_cos_sin(seq_len)
    ct = jnp.asarray(cos, dtype=BF16)
    st = jnp.asarray(-sin, dtype=BF16)
    tr = _tile(seq_len, 512, 8)
    tn = _tile(width, 1024 if seq_len <= 512 else 512, 128)
    n_rt = seq_len // tr
    tab = pl.BlockSpec((tr, seq_len), lambda r, s, j: (r, 0))
    src = pl.BlockSpec((seq_len, tn), lambda r, s, j: (sb0 + s, j))
    return pl.pallas_call(
        _fourier_pos_kernel,
        grid=(n_rt, n_seq, width // tn),
        in_specs=[tab, tab, src, src],
        out_specs=pl.BlockSpec((tr, tn), lambda r, s, j: (s * n_rt + r, j)),
        out_shape=jax.ShapeDtypeStruct((n_seq * seq_len, width), BF16),
        compiler_params=_cparams(("arbitrary", "arbitrary", "arbitrary"),
                                 8 * tr * seq_len + 8 * seq_len * tn + 12 * tr * tn + (8 << 20)),
        name="fourier_pos",
    )(ct, st, a_cos, a_sin)


def _rope_tables(seq_len):
    half = HEAD_DIM // 2
    t = np.arange(seq_len)
    inv_freq = (1.0 / (np.float32(ROPE_BASE) ** (np.arange(0, half, 2, dtype=np.float32) / np.float32(half)))).astype(np.float32)
    ang_row = ((t // GRID_W).astype(np.float32)[:, None] * inv_freq).astype(np.float64)
    ang_col = ((t % GRID_W).astype(np.float32)[:, None] * inv_freq).astype(np.float64)
    cos = np.concatenate([np.cos(ang_row)] * 2 + [np.cos(ang_col)] * 2, axis=1)
    sin = np.concatenate([-np.sin(ang_row), np.sin(ang_row), -np.sin(ang_col), np.sin(ang_col)], axis=1)
    return jnp.asarray(cos, dtype=F32), jnp.asarray(sin, dtype=F32)


def _gate_layouts(p_ab, heads):
    nt = p_ab.shape[0]
    per_head = p_ab[:, :4 * heads].reshape(nt // GROUP, GROUP, 4, heads)
    return per_head.transpose(3, 0, 1, 2), per_head.transpose(3, 0, 2, 1)


def kernel(x_prompt, x_sample, cache_na_k, cache_na_v, state_gdn, c, c_ctx, norm1_w, norm2_w, ada_w, ada_b, in_w, conv_w, gdn_a_log, gdn_dt_bias, gdn_norm_w, na_q_norm_w, na_k_norm_w, na_rpb, out_w, ffn_gate_w, ffn_up_w, ffn_down_w):
    batch, seq, d = x_prompt.shape
    dec_batch, dec_seq, _ = x_sample.shape
    depth = in_w.shape[0]
    gh = state_gdn.shape[3]
    nh = cache_na_k.shape[3]
    gw, nw = gh * HEAD_DIM, nh * HEAD_DIM
    fw = in_w.shape[2] - 4 * gw - 4 * gh - 3 * nw
    fg = fw // HEAD_DIM
    n_ctx, n_lat = batch * seq, dec_batch * dec_seq
    off_ab = 4 * gw
    off_na = off_ab + 4 * gh
    off_fn = off_na + 3 * nw
    assert dec_batch + 1 <= MOD_ROWS and 4 * gh <= HEAD_DIM
    win_rows = min(NA_ROWS, dec_seq // GRID_W)

    cond = jnp.concatenate([c_ctx[None, :], c, jnp.zeros((MOD_ROWS - 1 - dec_batch, d), F32)], axis=0)
    mods_all = _adaln(cond, ada_w, ada_b)
    rope_tabs = _rope_tables(dec_seq)

    x = jnp.concatenate([x_prompt.reshape(n_ctx, d), x_sample.reshape(n_lat, d)], axis=0)
    new_k, new_v, new_s = [], [], []
    for l in range(depth):
        mods = mods_all[l].reshape(MOD_ROWS * 6, 1, d)
        w_in = in_w[l]
        w_ab = jnp.pad(w_in[:, off_ab:off_na], ((0, 0), (0, HEAD_DIM - 4 * gh))).astype(BF16)
        w_na = w_in[:, off_na:off_fn].astype(BF16)
        w_fn = w_in[:, off_fn:].astype(BF16)

        h = _modulate(x, norm1_w[l], mods, 0, 1, n_ctx, dec_seq)
        p_gdn = _matmul(h, w_in[:, :off_ab].astype(BF16), F32, "proj_gdn")
        p_ab = _matmul(h, w_ab, F32, "proj_gates")
        p_na = _matmul(h, w_na, F32, "proj_na")
        p_fn = _matmul(h, w_fn, F32, "proj_fn")

        gate_col, gate_row = _gate_layouts(p_ab, gh)
        gdn_common = (p_gdn, gate_col, gate_row, conv_w[l], gdn_a_log[l], gdn_dt_bias[l], gdn_norm_w[l])
        gdn_ctx, s_ctx = _gdn(*gdn_common, row0=0, n_seq=batch, seq_len=seq, heads=gh, emit_state=True)
        gdn_lat = _gdn(*gdn_common, row0=n_ctx, n_seq=dec_batch, seq_len=dec_seq, heads=gh,
                       rope_tabs=rope_tabs, s0=state_gdn, layer=l)

        nk_ctx, nv_ctx, na_ctx = _ctx_attention(p_na, na_q_norm_w[l], na_k_norm_w[l], n_seq=batch, seq_len=seq, heads=nh)
        bias = _bias_table(na_rpb[l], win_rows)
        na_lat = _lat_attention(p_na, cache_na_k, cache_na_v, na_q_norm_w[l], na_k_norm_w[l], bias,
                                row0=n_ctx, n_seq=dec_batch, seq_len=dec_seq, heads=nh, layer=l)

        a_cos, a_sin = _fourier_chan(p_fn, fg)
        fn_ctx = _fourier_pos(a_cos, a_sin, row0=0, n_seq=batch, seq_len=seq)
        fn_lat = _fourier_pos(a_cos, a_sin, row0=n_ctx, n_seq=dec_batch, seq_len=dec_seq)

        x = _outproj([gdn_ctx, na_ctx, fn_ctx], [gdn_lat, na_lat, fn_lat], out_w[l].astype(BF16),
                     x, mods, 2, n_ctx, dec_seq)

        h2 = _modulate(x, norm2_w[l], mods, 3, 4, n_ctx, dec_seq)
        act = _swiglu(h2, ffn_gate_w, ffn_up_w, l)
        w_down = ffn_down_w[l].astype(BF16)
        if l < depth - 1:
            x = _down(act, w_down, x, mods, 5, n_ctx, dec_seq)
        else:
            y_ctx = _down(act, w_down, x, mods, 5, n_ctx, dec_seq, 0, n_ctx)
            y_lat = _down(act, w_down, x, mods, 5, n_ctx, dec_seq, n_ctx, n_lat)

        new_k.append(nk_ctx.reshape(batch, seq, nh, HEAD_DIM))
        new_v.append(nv_ctx.reshape(batch, seq, nh, HEAD_DIM))
        new_s.append(s_ctx)

    y_prompt = y_ctx.reshape(batch, seq, d)
    y_sample = y_lat.reshape(dec_batch, dec_seq, d)
    return (y_prompt, y_sample, jnp.stack(new_k, axis=1), jnp.stack(new_v, axis=1), jnp.stack(new_s, axis=1))
```

```python
import functools
import math

import jax
import jax.numpy as jnp
import numpy as np
from jax import lax
from jax.experimental import pallas as pl
from jax.experimental.pallas import tpu as pltpu

F32 = jnp.float32
BF16 = jnp.bfloat16

HEAD_DIM = 128
CONV_K = 5
CHUNK = 64
GROUP = 256
GRID_W = 64
NA_ROWS = 8
NA_COLS = 16
ROPE_BASE = 10000.0
EPS = 1e-6
MOD_ROWS = 8
CONV_PAD = 8
GDN_HEADS_PER_STEP = 2
V7X_VMEM_BYTES = 64 * 1024 * 1024
VMEM_CAP = V7X_VMEM_BYTES - 8 * 1024 * 1024


def _cparams(sem, vmem_bytes):
    limit = int(min(max(vmem_bytes, 16 * 1024 * 1024), VMEM_CAP))
    return pltpu.CompilerParams(dimension_semantics=sem, vmem_limit_bytes=limit)


def _tile(n, pref, align):
    if n <= pref:
        return n
    t = (pref // align) * align
    while t >= align:
        if n % t == 0:
            return t
        t -= align
    return n


def _sigmoid(x):
    return 1.0 / (1.0 + jnp.exp(-x))


def _silu(x):
    return x * _sigmoid(x)


def _softplus(x):
    return jnp.maximum(x, 0.0) + jnp.log(1.0 + jnp.exp(-jnp.abs(x)))


def _dot(a, b):
    return jnp.dot(a, b, preferred_element_type=F32)


def _dot_nt(a, b):
    return lax.dot_general(a, b, (((1,), (1,)), ((), ())), preferred_element_type=F32)


def _adaln_kernel(c_ref, w_ref, b_ref, o_ref):
    s = _silu(c_ref[...]).astype(BF16)
    o_ref[...] = _dot(s, w_ref[...].astype(BF16)) + b_ref[...]


def _adaln(cond, ada_w, ada_b):
    depth, d, n = ada_w.shape
    tn = _tile(n, 512, 128)
    return pl.pallas_call(
        _adaln_kernel,
        grid=(depth, n // tn),
        in_specs=[pl.BlockSpec((MOD_ROWS, d), lambda l, j: (0, 0)),
                  pl.BlockSpec((None, d, tn), lambda l, j: (l, 0, j)),
                  pl.BlockSpec((None, 1, tn), lambda l, j: (l, 0, j))],
        out_specs=pl.BlockSpec((None, MOD_ROWS, tn), lambda l, j: (l, 0, j)),
        out_shape=jax.ShapeDtypeStruct((depth, MOD_ROWS, n), F32),
        compiler_params=_cparams(("parallel", "arbitrary"), 3 * d * tn * 4 + (4 << 20)),
        name="adaln",
    )(cond, ada_w, ada_b.reshape(depth, 1, n))


class _Rows:
    def __init__(self, n_ctx, lat_seq, tm):
        assert n_ctx % tm == 0 and lat_seq % tm == 0
        self.ctx_tiles = n_ctx // tm
        self.seq_tiles = lat_seq // tm

    def mod_index(self, i, which):
        row = jnp.where(i < self.ctx_tiles, 0, 1 + (i - self.ctx_tiles) // self.seq_tiles)
        return row * 6 + which


def _modulate_kernel(*refs, ctx_tiles):
    x_refs, (g_ref, sh_ref, sc_ref, o_ref) = refs[:-4], refs[-4:]

    def run(x_ref):
        x = x_ref[...]
        y = x * lax.rsqrt(jnp.mean(x * x, axis=-1, keepdims=True) + EPS) * g_ref[...]
        o_ref[...] = (y * (1.0 + sc_ref[...]) + sh_ref[...]).astype(o_ref.dtype)

    if len(x_refs) == 1:
        run(x_refs[0])
        return
    is_ctx = pl.program_id(0) < ctx_tiles

    @pl.when(is_ctx)
    def _():
        run(x_refs[0])

    @pl.when(jnp.logical_not(is_ctx))
    def _():
        run(x_refs[1])


def _split_row_specs(shape, ct, lt, grid_rank):
    if grid_rank == 2:
        return [pl.BlockSpec(shape, lambda i, j: (jnp.minimum(i, ct - 1), j)),
                pl.BlockSpec(shape, lambda i, j: (jnp.clip(i - ct, 0, lt - 1), j))]
    return [pl.BlockSpec(shape, lambda i: (jnp.minimum(i, ct - 1), 0)),
            pl.BlockSpec(shape, lambda i: (jnp.clip(i - ct, 0, lt - 1), 0))]


def _modulate(x, gain, mods, shift_id, scale_id, n_ctx, lat_seq):
    xs = x if isinstance(x, tuple) else (x,)
    nt, d = sum(a.shape[0] for a in xs), xs[0].shape[1]
    tm = _tile(math.gcd(n_ctx, lat_seq), 256, 8)
    rows = _Rows(n_ctx, lat_seq, tm)
    if len(xs) == 1:
        x_specs = [pl.BlockSpec((tm, d), lambda i: (i, 0))]
    else:
        x_specs = _split_row_specs((tm, d), rows.ctx_tiles, (nt - n_ctx) // tm, 1)
    return pl.pallas_call(
        functools.partial(_modulate_kernel, ctx_tiles=rows.ctx_tiles),
        grid=(nt // tm,),
        in_specs=x_specs + [
                  pl.BlockSpec((1, d), lambda i: (0, 0)),
                  pl.BlockSpec((None, 1, d), lambda i: (rows.mod_index(i, shift_id), 0, 0)),
                  pl.BlockSpec((None, 1, d), lambda i: (rows.mod_index(i, scale_id), 0, 0))],
        out_specs=pl.BlockSpec((tm, d), lambda i: (i, 0)),
        out_shape=jax.ShapeDtypeStruct((nt, d), BF16),
        compiler_params=_cparams(("parallel",), 28 * tm * d + (4 << 20)),
        name="modulate",
    )(*xs, gain.reshape(1, d), mods, mods)


def _mm_kernel(a_ref, w_ref, o_ref):
    o_ref[...] = _dot(a_ref[...], w_ref[...].astype(BF16)).astype(o_ref.dtype)


def _matmul(a, w, out_dtype, name, layer=None, n_cols=None):
    m, k = a.shape
    n = w.shape[-1] if n_cols is None else n_cols
    tm = _tile(m, 1024, 8)
    tn = _tile(n, 512, 128)
    if layer is None:
        w_spec = pl.BlockSpec((k, tn), lambda i, j: (0, j))
    else:
        w_spec = pl.BlockSpec((None, k, tn), lambda i, j: (layer, 0, j))
    wbytes = jnp.dtype(w.dtype).itemsize
    return pl.pallas_call(
        _mm_kernel,
        grid=(m // tm, n // tn),
        in_specs=[pl.BlockSpec((tm, k), lambda i, j: (i, 0)), w_spec],
        out_specs=pl.BlockSpec((tm, tn), lambda i, j: (i, j)),
        out_shape=jax.ShapeDtypeStruct((m, n), out_dtype),
        compiler_params=_cparams(("parallel", "arbitrary"),
                                 4 * tm * k + (2 * wbytes + 2) * k * tn + 12 * tm * tn + (4 << 20)),
        name=name,
    )(a, w)


def _cast_kernel(w_ref, o_ref):
    o_ref[...] = w_ref[...].astype(o_ref.dtype)


def _cast_layer(w, layer):
    _, k, n = w.shape
    tr = _tile(k, 512, 8)
    return pl.pallas_call(
        _cast_kernel,
        grid=(k // tr,),
        in_specs=[pl.BlockSpec((None, tr, n), lambda i: (layer, i, 0))],
        out_specs=pl.BlockSpec((tr, n), lambda i: (i, 0)),
        out_shape=jax.ShapeDtypeStruct((k, n), BF16),
        compiler_params=_cparams(("parallel",), 12 * tr * n + (4 << 20)),
        name="cast_weight",
    )(w)


def _outproj_kernel(*refs, n_parts, widths, kb, ctx_tiles):
    ctx_refs = refs[:n_parts]
    lat_refs = refs[n_parts:2 * n_parts]
    n_w = sum(widths) // kb
    w_refs = refs[2 * n_parts:2 * n_parts + n_w]
    x_refs, (g_ref, o_ref) = refs[2 * n_parts + n_w:-2], refs[-2:]

    def run(a_refs, x_ref):
        acc = None
        blk = 0
        for a_ref, width in zip(a_refs, widths):
            for c in range(width // kb):
                term = _dot(a_ref[:, c * kb:(c + 1) * kb], w_refs[blk][...])
                acc = term if acc is None else acc + term
                blk += 1
        o_ref[...] = x_ref[...] + g_ref[...] * acc

    is_ctx = pl.program_id(0) < ctx_tiles

    @pl.when(is_ctx)
    def _():
        run(ctx_refs, x_refs[0])

    @pl.when(jnp.logical_not(is_ctx))
    def _():
        run(lat_refs, x_refs[-1])


def _outproj(ctx_parts, lat_parts, w_out, x, mods, gate_id, n_ctx, lat_seq):
    xs = x if isinstance(x, tuple) else (x,)
    nt, d = sum(a.shape[0] for a in xs), xs[0].shape[1]
    tm = _tile(math.gcd(n_ctx, lat_seq), 1024, 8)
    tn = _tile(d, 512, 128)
    rows = _Rows(n_ctx, lat_seq, tm)
    widths = tuple(p.shape[1] for p in ctx_parts)
    kb = functools.reduce(math.gcd, widths)
    ktot = sum(widths)
    ct = rows.ctx_tiles
    lt = (nt - n_ctx) // tm
    once = pl.Buffered(1)
    a_specs = [pl.BlockSpec((tm, w), lambda i, j: (jnp.minimum(i, ct - 1), 0), pipeline_mode=once)
               for w in widths]
    a_specs += [pl.BlockSpec((tm, w), lambda i, j: (jnp.clip(i - ct, 0, lt - 1), 0), pipeline_mode=once)
                for w in widths]
    w_specs = [pl.BlockSpec((kb, tn), lambda i, j, b=b: (b, j)) for b in range(ktot // kb)]
    if len(xs) == 1:
        x_specs = [pl.BlockSpec((tm, tn), lambda i, j: (i, j))]
    else:
        x_specs = _split_row_specs((tm, tn), ct, lt, 2)
    return pl.pallas_call(
        functools.partial(_outproj_kernel, n_parts=len(widths), widths=widths, kb=kb, ctx_tiles=ct),
        grid=(nt // tm, d // tn),
        in_specs=a_specs + w_specs + x_specs + [
            pl.BlockSpec((None, 1, tn), lambda i, j: (rows.mod_index(i, gate_id), 0, j))],
        out_specs=pl.BlockSpec((tm, tn), lambda i, j: (i, j)),
        out_shape=jax.ShapeDtypeStruct((nt, d), F32),
        compiler_params=_cparams(("parallel", "arbitrary"),
                                 4 * tm * ktot + 4 * ktot * tn + 32 * tm * tn + (4 << 20)),
        name="outproj",
    )(*ctx_parts, *lat_parts, *([w_out] * (ktot // kb)), *xs, mods)


def _swiglu_kernel(h_ref, wg_ref, wu_ref, o_ref):
    h = h_ref[...]
    g = _dot(h, wg_ref[...].astype(BF16))
    u = _dot(h, wu_ref[...].astype(BF16))
    o_ref[...] = (_silu(g) * u).astype(o_ref.dtype)


def _swiglu(h, wg, wu, layer):
    m, k = h.shape
    n = wg.shape[2]
    tm = _tile(m, 2048, 8)
    tn = _tile(n, 256, 128)
    w_spec = pl.BlockSpec((None, k, tn), lambda i, j: (layer, 0, j))
    return pl.pallas_call(
        _swiglu_kernel,
        grid=(m // tm, n // tn),
        in_specs=[pl.BlockSpec((tm, k), lambda i, j: (i, 0), pipeline_mode=pl.Buffered(1)), w_spec, w_spec],
        out_specs=pl.BlockSpec((tm, tn), lambda i, j: (i, j)),
        out_shape=jax.ShapeDtypeStruct((m, n), BF16),
        compiler_params=_cparams(("parallel", "arbitrary"),
                                 2 * tm * k + 20 * k * tn + 24 * tm * tn + (4 << 20)),
        name="swiglu",
    )(h, wg, wu)


def _down_kernel(a_ref, w_ref, x_ref, g_ref, o_ref, acc_ref, *, k_steps):
    kk = pl.program_id(2)
    part = _dot(a_ref[...], w_ref[...])

    @pl.when(kk == 0)
    def _():
        acc_ref[...] = part

    @pl.when(kk > 0)
    def _():
        acc_ref[...] += part

    @pl.when(kk == k_steps - 1)
    def _():
        o_ref[...] = x_ref[...] + g_ref[...] * acc_ref[...]


def _down(a, w, x, mods, gate_id, n_ctx, lat_seq, row0=0, n_rows=None):
    nt, k = a.shape
    d = w.shape[1]
    n_rows = nt if n_rows is None else n_rows
    tm = _tile(math.gcd(n_ctx, lat_seq), 1024, 8)
    tn = _tile(d, 512, 128)
    tk = _tile(k, 6144, 128)
    rows = _Rows(n_ctx, lat_seq, tm)
    assert row0 % tm == 0 and n_rows % tm == 0
    t0 = row0 // tm
    return pl.pallas_call(
        functools.partial(_down_kernel, k_steps=k // tk),
        grid=(n_rows // tm, d // tn, k // tk),
        in_specs=[pl.BlockSpec((tm, tk), lambda i, j, s: (t0 + i, s)),
                  pl.BlockSpec((tk, tn), lambda i, j, s: (s, j)),
                  pl.BlockSpec((tm, tn), lambda i, j, s: (t0 + i, j)),
                  pl.BlockSpec((None, 1, tn), lambda i, j, s: (rows.mod_index(t0 + i, gate_id), 0, j))],
        out_specs=pl.BlockSpec((tm, tn), lambda i, j, s: (i, j)),
        out_shape=jax.ShapeDtypeStruct((n_rows, d), F32),
        scratch_shapes=[pltpu.VMEM((tm, tn), F32)],
        compiler_params=_cparams(("parallel", "arbitrary", "arbitrary"),
                                 4 * tm * tk + 4 * tk * tn + 28 * tm * tn + (4 << 20)),
        name="down",
    )(a, w, x, mods)


def _neumann_solve(strict, rhs):
    def split2(v):
        hi = v.astype(BF16)
        return hi, (v - hi.astype(F32)).astype(BF16)

    def dot3(a, b):
        return _dot(a[0], b[0]) + _dot(a[0], b[1]) + _dot(a[1], b[0])

    p = -strict
    x = rhs
    n_fac = int(math.log2(CHUNK))
    for f in range(n_fac):
        ps = split2(p)
        x = x + dot3(ps, split2(x))
        if f < n_fac - 1:
            p = dot3(ps, ps)
    return x


def _gdn_kernel(*refs, seq_len, hps, rotary, has_s0, emit_state):
    it = iter(refs)
    alog_ref, dtb_ref = next(it), next(it)
    q_ref, k_ref, v_ref, z_ref = next(it), next(it), next(it), next(it)
    cw_refs = (next(it), next(it), next(it))
    acol_ref, arow_ref, gw_ref = next(it), next(it), next(it)
    cos_ref = sin_ref = s0_ref = sf_ref = None
    if rotary:
        cos_ref, sin_ref = next(it), next(it)
    if has_s0:
        s0_ref = next(it)
    o_ref = next(it)
    if emit_state:
        sf_ref = next(it)
    pad_refs = (next(it), next(it), next(it))
    qs_ref, kn_ref, vc_ref, oacc_ref, s_ref = next(it), next(it), next(it), next(it), next(it)

    head0 = pl.program_id(1) * hps
    n_groups = seq_len // GROUP
    n_chunks = GROUP // CHUNK
    scale = HEAD_DIM ** -0.5
    width = hps * HEAD_DIM

    def cols(j):
        return slice(j * HEAD_DIM, (j + 1) * HEAD_DIM)

    for src, pad in zip((q_ref, k_ref, v_ref), pad_refs):
        pad[pl.ds(0, CONV_PAD), :] = jnp.zeros((CONV_PAD, width), F32)
        pad[pl.ds(CONV_PAD + seq_len, CONV_PAD), :] = jnp.zeros((CONV_PAD, width), F32)
        pad[pl.ds(CONV_PAD, seq_len), :] = src[...]

    lane = lax.broadcasted_iota(jnp.int32, (GROUP, HEAD_DIM), 1)
    first_half = (lane % 64) < 32

    def conv_silu(pad, cw_ref, g0, j):
        win = pad[pl.ds(g0, GROUP + 2 * CONV_PAD), cols(j)]
        cw = cw_ref[:, cols(j)]
        acc = None
        for tap in range(CONV_K):
            off = CONV_PAD - CONV_K // 2 + tap
            term = win[off:off + GROUP, :] * cw[tap:tap + 1, :]
            acc = term if acc is None else acc + term
        return _silu(acc)

    def l2n(x):
        return x * lax.rsqrt(jnp.sum(x * x, axis=-1, keepdims=True) + EPS)

    def rope(x, g0):
        cos = cos_ref[pl.ds(g0, GROUP), :]
        sin = sin_ref[pl.ds(g0, GROUP), :]
        swapped = jnp.where(first_half, pltpu.roll(x, HEAD_DIM - 32, axis=1), pltpu.roll(x, 32, axis=1))
        return x * cos + swapped * sin

    def prep(gi):
        g0 = pl.multiple_of(gi * GROUP, GROUP)
        for j in range(hps):
            q = l2n(conv_silu(pad_refs[0], cw_refs[0], g0, j))
            k = l2n(conv_silu(pad_refs[1], cw_refs[1], g0, j))
            v = conv_silu(pad_refs[2], cw_refs[2], g0, j)
            if rotary:
                q, k = rope(q, g0), rope(k, g0)
            qs_ref[pl.ds(g0, GROUP), cols(j)] = q * scale
            kn_ref[pl.ds(g0, GROUP), cols(j)] = k
            vc_ref[pl.ds(g0, GROUP), cols(j)] = v

    ri = lax.broadcasted_iota(jnp.int32, (GROUP, GROUP), 0)
    ci = lax.broadcasted_iota(jnp.int32, (GROUP, GROUP), 1)
    same = (ri // CHUNK) == (ci // CHUNK)
    tok_chunk = lax.broadcasted_iota(jnp.int32, (HEAD_DIM, GROUP), 1) // CHUNK
    sub8 = lax.broadcasted_iota(jnp.int32, (8, GROUP), 0)
    lane_chunk = lax.broadcasted_iota(jnp.int32, (1, GROUP), 1) // CHUNK

    def gate_sums(gi, j):
        ac = acol_ref[j, gi]
        ar = arow_ref[j, gi]
        g_c, g_r, beta = [], [], []
        for d in (0, 1):
            alog = alog_ref[d, head0 + j]
            dtb = dtb_ref[d, head0 + j]
            g_c.append(-jnp.exp(jnp.full((GROUP, 1), alog, F32)) * _softplus(ac[:, d:d + 1] + dtb))
            g_r.append(-jnp.exp(jnp.full((1, GROUP), alog, F32)) * _softplus(ar[d:d + 1, :] + dtb))
            beta.append(_sigmoid(ac[:, 2 + d:3 + d]))
        ones_low = jnp.where(same & (ci <= ri), 1.0, 0.0).astype(BF16)
        rem_c = jnp.where(lane == 0, g_c[0], jnp.where(lane == 1, g_c[1], 0.0))
        rem_r = jnp.where(sub8 == 0, g_r[0], jnp.where(sub8 == 1, g_r[1], 0.0))
        pre_c = jnp.zeros((GROUP, HEAD_DIM), F32)
        pre_r = jnp.zeros((8, GROUP), F32)
        for _ in range(3):
            piece_c = rem_c.astype(BF16)
            piece_r = rem_r.astype(BF16)
            pre_c = pre_c + _dot(ones_low, piece_c)
            pre_r = pre_r + _dot_nt(piece_r, ones_low)
            rem_c = rem_c - piece_c.astype(F32)
            rem_r = rem_r - piece_r.astype(F32)
        tot_c = jnp.concatenate(
            [jnp.broadcast_to(pre_c[c * CHUNK + CHUNK - 1:(c + 1) * CHUNK, 1:2], (CHUNK, 1))
             for c in range(n_chunks)], axis=0)
        tot_r = jnp.zeros((1, GROUP), F32)
        for c in range(n_chunks):
            tot_r = jnp.where(lane_chunk == c, pre_r[1:2, c * CHUNK + CHUNK - 1:(c + 1) * CHUNK], tot_r)
        cum_c = (pre_c[:, 0:1], tot_c - pre_c[:, 1:2] + g_c[1])
        cum_r = (pre_r[0:1, :], tot_r - pre_r[1:2, :] + g_r[1])
        return beta, cum_c, cum_r

    def solve_group(gi, direction, j, sums):
        fwd = direction == 0
        g0 = pl.multiple_of(gi * GROUP, GROUP)
        qs = qs_ref[pl.ds(g0, GROUP), cols(j)]
        k = kn_ref[pl.ds(g0, GROUP), cols(j)]
        v = vc_ref[pl.ds(g0, GROUP), cols(j)]
        beta = sums[0][direction]
        gc = sums[1][direction]
        gr = sums[2][direction]
        incl = same & ((ci <= ri) if fwd else (ci >= ri))
        strict = same & ((ci < ri) if fwd else (ci > ri))
        decay = jnp.exp(jnp.where(incl, gc - gr, -jnp.inf))
        kb = k * beta
        both = _dot_nt(jnp.concatenate([kb, qs], axis=0).astype(BF16), k.astype(BF16))
        tri = jnp.where(strict, both[:GROUP] * decay, 0.0)
        intra = (both[GROUP:] * decay).astype(BF16)
        eg = jnp.exp(gc)
        sol = _neumann_solve(tri, jnp.concatenate([v * beta, kb * eg], axis=1))
        k_dec_parts = []
        chunk_decay = []
        for c in range(n_chunks):
            edge = c * CHUNK + (CHUNK - 1 if fwd else 0)
            tot = gc[edge:edge + 1, :]
            chunk_decay.append(jnp.exp(tot))
            k_dec_parts.append(k[c * CHUNK:(c + 1) * CHUNK] * jnp.exp(tot - gc[c * CHUNK:(c + 1) * CHUNK]))
        k_dec_t = jnp.concatenate(k_dec_parts, axis=0).T
        sol_b = sol.astype(BF16)
        lhs = [jnp.where(tok_chunk == c, k_dec_t, 0.0).astype(BF16) for c in range(n_chunks)]
        res = _dot(jnp.concatenate(lhs + [intra], axis=0), sol_b)
        kt_uw = [res[c * HEAD_DIM:(c + 1) * HEAD_DIM] for c in range(n_chunks)]
        a_uw = res[n_chunks * HEAD_DIM:]
        q_eff = (qs * eg - a_uw[:, HEAD_DIM:]).astype(BF16)
        return q_eff, a_uw[:, :HEAD_DIM], kt_uw, chunk_decay

    def run_chain(direction, solved, state):
        q_eff, o_loc, kt_uw, chunk_decay = solved
        outs = [None] * n_chunks
        for c in (range(n_chunks) if direction == 0 else reversed(range(n_chunks))):
            r0, r1 = c * CHUNK, (c + 1) * CHUNK
            lhs = jnp.concatenate([q_eff[r0:r1], kt_uw[c][:, HEAD_DIM:].astype(BF16)], axis=0)
            res = _dot(lhs, state.astype(BF16))
            outs[c] = res[:CHUNK] + o_loc[r0:r1]
            state = state * chunk_decay[c] + kt_uw[c][:, :HEAD_DIM] - res[CHUNK:]
        return jnp.concatenate(outs, axis=0), state

    def finish(o, g0, j):
        y = o * lax.rsqrt(jnp.mean(o * o, axis=-1, keepdims=True) + EPS) * gw_ref[...]
        o_ref[pl.ds(g0, GROUP), cols(j)] = (y * _silu(z_ref[pl.ds(g0, GROUP), cols(j)])).astype(o_ref.dtype)

    def init_state(direction, j):
        if has_s0:
            return s0_ref[direction, j]
        return jnp.zeros((HEAD_DIM, HEAD_DIM), F32)

    if n_groups == 1:
        prep(0)
        for j in range(hps):
            sums = gate_sums(0, j)
            o_f, s_f = run_chain(0, solve_group(0, 0, j, sums), init_state(0, j))
            o_b, s_b = run_chain(1, solve_group(0, 1, j, sums), init_state(1, j))
            if emit_state:
                sf_ref[0, j] = s_f
                sf_ref[1, j] = s_b
            finish(o_f + o_b, 0, j)
        return

    def prep_body(gi, carry):
        prep(gi)
        oacc_ref[pl.ds(pl.multiple_of(gi * GROUP, GROUP), GROUP), :] = jnp.zeros((GROUP, width), F32)
        return carry

    lax.fori_loop(0, n_groups, prep_body, 0)
    for j in range(hps):
        s_ref[0, j] = init_state(0, j)
        s_ref[1, j] = init_state(1, j)

    def scan_body(t, carry):
        gf, gb = t, n_groups - 1 - t
        rf = pl.ds(pl.multiple_of(gf * GROUP, GROUP), GROUP)
        rb = pl.ds(pl.multiple_of(gb * GROUP, GROUP), GROUP)
        for j in range(hps):
            o_f, s_f = run_chain(0, solve_group(gf, 0, j, gate_sums(gf, j)), s_ref[0, j])
            o_b, s_b = run_chain(1, solve_group(gb, 1, j, gate_sums(gb, j)), s_ref[1, j])
            s_ref[0, j] = s_f
            s_ref[1, j] = s_b
            oacc_ref[rf, cols(j)] += o_f
            oacc_ref[rb, cols(j)] += o_b
        return carry

    lax.fori_loop(0, n_groups, scan_body, 0)
    if emit_state:
        sf_ref[...] = s_ref[...]

    def finish_body(gi, carry):
        g0 = pl.multiple_of(gi * GROUP, GROUP)
        for j in range(hps):
            finish(oacc_ref[pl.ds(g0, GROUP), cols(j)], g0, j)
        return carry

    lax.fori_loop(0, n_groups, finish_body, 0)


def _gdn(p_gdn, gate_col, gate_row, conv_w, a_log, dt_bias, norm_w, *, row0, n_seq, seq_len,
         heads, rope_tabs=None, s0=None, layer=None, emit_state=False):
    assert row0 % seq_len == 0 and seq_len % GROUP == 0
    hps = _tile(heads, GDN_HEADS_PER_STEP, 1)
    hb = heads // hps
    width = hps * HEAD_DIM
    sb0 = row0 // seq_len
    gps = seq_len // GROUP
    rotary = rope_tabs is not None
    smem = pl.BlockSpec(memory_space=pltpu.SMEM)

    def col(off):
        return pl.BlockSpec((seq_len, width), lambda s, h: (sb0 + s, off + h))

    def cw(off):
        return pl.BlockSpec((CONV_K, width), lambda s, h: (0, off + h))

    in_specs = [smem, smem, col(0), col(hb), col(2 * hb), col(3 * hb),
                cw(0), cw(hb), cw(2 * hb),
                pl.BlockSpec((hps, gps, GROUP, 4), lambda s, h: (h, sb0 + s, 0, 0)),
                pl.BlockSpec((hps, gps, 4, GROUP), lambda s, h: (h, sb0 + s, 0, 0)),
                pl.BlockSpec((1, HEAD_DIM), lambda s, h: (0, 0))]
    args = [a_log, dt_bias, p_gdn, p_gdn, p_gdn, p_gdn, conv_w, conv_w, conv_w,
            gate_col, gate_row, norm_w.reshape(1, HEAD_DIM)]
    if rotary:
        in_specs += [pl.BlockSpec((seq_len, HEAD_DIM), lambda s, h: (0, 0))] * 2
        args += list(rope_tabs)
    if s0 is not None:
        in_specs.append(pl.BlockSpec((None, None, 2, hps, HEAD_DIM, HEAD_DIM),
                                     lambda s, h: (s, layer, 0, h, 0, 0)))
        args.append(s0)
    out_shape = [jax.ShapeDtypeStruct((n_seq * seq_len, heads * HEAD_DIM), BF16)]
    out_specs = [pl.BlockSpec((seq_len, width), lambda s, h: (s, h))]
    if emit_state:
        out_shape.append(jax.ShapeDtypeStruct((n_seq, 2, heads, HEAD_DIM, HEAD_DIM), F32))
        out_specs.append(pl.BlockSpec((None, 2, hps, HEAD_DIM, HEAD_DIM), lambda s, h: (s, 0, h, 0, 0)))
    scratch = [pltpu.VMEM((seq_len + 2 * CONV_PAD, width), F32)] * 3
    scratch += [pltpu.VMEM((seq_len, width), F32)] * 4
    scratch += [pltpu.VMEM((2, hps, HEAD_DIM, HEAD_DIM), F32)]
    res = pl.pallas_call(
        functools.partial(_gdn_kernel, seq_len=seq_len, hps=hps, rotary=rotary, has_s0=s0 is not None,
                          emit_state=emit_state),
        grid=(n_seq, hb),
        in_specs=in_specs, out_specs=out_specs, out_shape=out_shape, scratch_shapes=scratch,
        compiler_params=_cparams(("parallel", "arbitrary"), 24 * seq_len * width * 4 + (16 << 20)),
        name="gdn_latent" if rotary else "gdn_context",
    )(*args)
    return res if emit_state else res[0]


def _head_rms(x, w):
    return x * lax.rsqrt(jnp.mean(x * x, axis=-1, keepdims=True) + EPS) * w


def _softmax_rows(parts):
    m = None
    for s in parts:
        mi = jnp.max(s, axis=-1, keepdims=True)
        m = mi if m is None else jnp.maximum(m, mi)
    ex = [jnp.exp(s - m) for s in parts]
    den = None
    for e in ex:
        si = jnp.sum(e, axis=-1, keepdims=True)
        den = si if den is None else den + si
    inv = 1.0 / den
    return [e * inv for e in ex]


def _ctx_attn_kernel(q_ref, k_ref, v_ref, qw_ref, kw_ref, nk_ref, nv_ref, o_ref, *, heads_per_step):
    for j in range(heads_per_step):
        cols = slice(j * HEAD_DIM, (j + 1) * HEAD_DIM)
        q = _head_rms(q_ref[:, cols], qw_ref[...])
        k = _head_rms(k_ref[:, cols], kw_ref[...])
        v = v_ref[:, cols]
        nk_ref[:, cols] = k
        nv_ref[:, cols] = v
        s = _dot_nt(q.astype(BF16), k.astype(BF16)) * (HEAD_DIM ** -0.5)
        (p,) = _softmax_rows([s])
        o_ref[:, cols] = _dot(p.astype(BF16), v.astype(BF16)).astype(o_ref.dtype)


def _ctx_attention(p_na, q_w, k_w, *, n_seq, seq_len, heads):
    hps = _tile(heads, 8, 1)
    hb = heads // hps
    width = hps * HEAD_DIM

    def col(off):
        return pl.BlockSpec((seq_len, width), lambda s, h: (s, off + h))

    wspec = pl.BlockSpec((1, HEAD_DIM), lambda s, h: (0, 0))
    rows = n_seq * seq_len
    return pl.pallas_call(
        functools.partial(_ctx_attn_kernel, heads_per_step=hps),
        grid=(n_seq, hb),
        in_specs=[col(0), col(hb), col(2 * hb), wspec, wspec],
        out_specs=[col(0), col(0), col(0)],
        out_shape=[jax.ShapeDtypeStruct((rows, heads * HEAD_DIM), F32),
                   jax.ShapeDtypeStruct((rows, heads * HEAD_DIM), F32),
                   jax.ShapeDtypeStruct((rows, heads * HEAD_DIM), BF16)],
        compiler_params=_cparams(("parallel", "arbitrary"), 48 * seq_len * width + (8 << 20)),
        name="ctx_attention",
    )(p_na, p_na, p_na, q_w.reshape(1, HEAD_DIM), k_w.reshape(1, HEAD_DIM))


def _bias_kernel(rpb_ref, o_ref, *, win_rows):
    head = pl.program_id(0)
    n_dr = 2 * NA_ROWS - 1
    n_dc = 2 * NA_COLS - 1
    qc = lax.broadcasted_iota(jnp.int32, (GRID_W, GRID_W), 0)
    kc = lax.broadcasted_iota(jnp.int32, (GRID_W, GRID_W), 1)
    dc = jnp.clip(kc - qc, -(NA_COLS - 1), NA_COLS - 1) + (NA_COLS - 1)
    start = jnp.clip(qc - NA_COLS // 2, 0, GRID_W - NA_COLS)
    in_win = (kc >= start) & (kc < start + NA_COLS)
    blocks = {}
    for e in range(win_rows):
        for j in range(win_rows):
            dr = j - e + (NA_ROWS - 1)
            if dr in blocks:
                continue
            acc = jnp.zeros((GRID_W, GRID_W), F32)
            for d in range(n_dc):
                acc = jnp.where(dc == d, rpb_ref[(head * n_dr + dr) * n_dc + d], acc)
            blocks[dr] = jnp.where(in_win, acc, -jnp.inf)
    for e in range(win_rows):
        o_ref[e] = jnp.concatenate([blocks[j - e + (NA_ROWS - 1)] for j in range(win_rows)], axis=1)


def _bias_table(rpb, win_rows):
    heads = rpb.shape[0]
    return pl.pallas_call(
        functools.partial(_bias_kernel, win_rows=win_rows),
        grid=(heads,),
        in_specs=[pl.BlockSpec(memory_space=pltpu.SMEM)],
        out_specs=pl.BlockSpec((None, win_rows, GRID_W, win_rows * GRID_W), lambda h: (h, 0, 0, 0)),
        out_shape=jax.ShapeDtypeStruct((heads, win_rows, GRID_W, win_rows * GRID_W), F32),
        compiler_params=_cparams(("arbitrary",), 16 << 20),
        name="na_bias",
    )(rpb.reshape(-1))


def _lat_attn_kernel(q_ref, k_ref, v_ref, ck_ref, cv_ref, qw_ref, kw_ref, bias_ref, o_ref,
                     qn_ref, kn_ref, vb_ref, octx_ref, mctx_ref, lctx_ref, *, seq_len, win_rows, unroll):
    rows = seq_len // GRID_W
    n_blk = seq_len // GROUP
    band = win_rows * GRID_W
    scale = HEAD_DIM ** -0.5
    ck = ck_ref[...].astype(BF16)
    cv = cv_ref[...].astype(BF16)

    def prep(bi, carry):
        r0 = pl.multiple_of(bi * GROUP, GROUP)
        qn = _head_rms(q_ref[pl.ds(r0, GROUP), :], qw_ref[...]).astype(BF16)
        qn_ref[pl.ds(r0, GROUP), :] = qn
        kn_ref[pl.ds(r0, GROUP), :] = _head_rms(k_ref[pl.ds(r0, GROUP), :], kw_ref[...]).astype(BF16)
        vb_ref[pl.ds(r0, GROUP), :] = v_ref[pl.ds(r0, GROUP), :].astype(BF16)
        s_ctx = _dot_nt(qn, ck) * scale
        m_ctx = jnp.max(s_ctx, axis=-1, keepdims=True)
        e_ctx = jnp.exp(s_ctx - m_ctx)
        mctx_ref[pl.ds(r0, GROUP), :] = m_ctx
        lctx_ref[pl.ds(r0, GROUP), :] = jnp.sum(e_ctx, axis=-1, keepdims=True)
        octx_ref[pl.ds(r0, GROUP), :] = _dot(e_ctx.astype(BF16), cv)
        return carry

    lax.fori_loop(0, n_blk, prep, 0)

    def body(r, carry):
        rs = jnp.clip(r - win_rows // 2, 0, rows - win_rows)
        q0 = pl.multiple_of(r * GRID_W, GRID_W)
        k0 = pl.multiple_of(rs * GRID_W, GRID_W)
        q = qn_ref[pl.ds(q0, GRID_W), :]
        s_loc = _dot_nt(q, kn_ref[pl.ds(k0, band), :]) * scale + bias_ref[r - rs]
        m_ctx = mctx_ref[pl.ds(q0, GRID_W), :]
        m = jnp.maximum(jnp.max(s_loc, axis=-1, keepdims=True), m_ctx)
        e_loc = jnp.exp(s_loc - m)
        w_ctx = jnp.exp(m_ctx - m)
        den = jnp.sum(e_loc, axis=-1, keepdims=True) + w_ctx * lctx_ref[pl.ds(q0, GRID_W), :]
        num = _dot(e_loc.astype(BF16), vb_ref[pl.ds(k0, band), :]) + w_ctx * octx_ref[pl.ds(q0, GRID_W), :]
        o_ref[pl.ds(q0, GRID_W), :] = (num * (1.0 / den)).astype(o_ref.dtype)
        return carry

    lax.fori_loop(0, rows, body, 0, unroll=unroll)


def _lat_attention(p_na, cache_k, cache_v, q_w, k_w, bias, *, row0, n_seq, seq_len, heads, layer):
    assert row0 % seq_len == 0 and seq_len % GROUP == 0
    sb0 = row0 // seq_len
    past = cache_k.shape[2]
    win_rows = bias.shape[1]
    rows = seq_len // GRID_W

    def col(off):
        return pl.BlockSpec((seq_len, HEAD_DIM), lambda s, h: (sb0 + s, off + h))

    cache = pl.BlockSpec((None, None, past, HEAD_DIM), lambda s, h: (s, layer, 0, h))
    wspec = pl.BlockSpec((1, HEAD_DIM), lambda s, h: (0, 0))
    return pl.pallas_call(
        functools.partial(_lat_attn_kernel, seq_len=seq_len, win_rows=win_rows,
                          unroll=4 if rows % 4 == 0 else 1),
        grid=(n_seq, heads),
        in_specs=[col(0), col(heads), col(2 * heads), cache, cache, wspec, wspec,
                  pl.BlockSpec((None, win_rows, GRID_W, win_rows * GRID_W), lambda s, h: (h, 0, 0, 0))],
        out_specs=pl.BlockSpec((seq_len, HEAD_DIM), lambda s, h: (s, h)),
        out_shape=jax.ShapeDtypeStruct((n_seq * seq_len, heads * HEAD_DIM), BF16),
        scratch_shapes=[pltpu.VMEM((seq_len, HEAD_DIM), BF16)] * 3 + [pltpu.VMEM((seq_len, HEAD_DIM), F32)]
        + [pltpu.VMEM((seq_len, 1), F32)] * 2,
        compiler_params=_cparams(("parallel", "arbitrary"), 16 * seq_len * HEAD_DIM * 4 + (16 << 20)),
        name="lat_attention",
    )(p_na, p_na, p_na, cache_k.reshape(cache_k.shape[:3] + (-1,)), cache_v.reshape(cache_v.shape[:3] + (-1,)),
      q_w.reshape(1, HEAD_DIM), k_w.reshape(1, HEAD_DIM), bias)


def _dft_cos_sin(n):
    idx = np.arange(n, dtype=np.int64)
    ang = ((idx[:, None] * idx[None, :]) % n).astype(np.float64) * (2.0 * math.pi / n)
    return (np.cos(ang) / math.sqrt(n)).astype(np.float32), (np.sin(ang) / math.sqrt(n)).astype(np.float32)


def _fourier_chan_kernel(x_ref, tab_ref, ac_ref, as_ref, *, groups):
    tab = tab_ref[...]
    for g in range(groups):
        cols = slice(g * HEAD_DIM, (g + 1) * HEAD_DIM)
        a = _dot(x_ref[:, cols].astype(BF16), tab)
        ac_ref[:, cols] = a[:, :HEAD_DIM].astype(BF16)
        as_ref[:, cols] = a[:, HEAD_DIM:].astype(BF16)


def _fourier_chan(p_fn, groups):
    nt, width = p_fn.shape
    cos, sin = _dft_cos_sin(HEAD_DIM)
    tab = jnp.asarray(np.concatenate([cos, sin], axis=1), dtype=BF16)
    tm = _tile(nt, 512, 8)
    blk = pl.BlockSpec((tm, width), lambda i: (i, 0))
    return pl.pallas_call(
        functools.partial(_fourier_chan_kernel, groups=groups),
        grid=(nt // tm,),
        in_specs=[blk, pl.BlockSpec((HEAD_DIM, 2 * HEAD_DIM), lambda i: (0, 0))],
        out_specs=[blk, blk],
        out_shape=[jax.ShapeDtypeStruct((nt, width), BF16)] * 2,
        compiler_params=_cparams(("parallel",), 24 * tm * width + (8 << 20)),
        name="fourier_chan",
    )(p_fn, tab)


def _fourier_pos_kernel(ct_ref, st_ref, ac_ref, as_ref, o_ref):
    o_ref[...] = (_dot(ct_ref[...], ac_ref[...]) + _dot(st_ref[...], as_ref[...])).astype(o_ref.dtype)


def _fourier_pos(a_cos, a_sin, *, row0, n_seq, seq_len):
    assert row0 % seq_len == 0
    sb0 = row0 // seq_len
    width = a_cos.shape[1]
    cos, sin = _dft_cos_sin(seq_len)
    ct = jnp.asarray(cos, dtype=BF16)
    st = jnp.asarray(-sin, dtype=BF16)
    tr = _tile(seq_len, 512, 8)
    tn = _tile(width, 1024 if seq_len <= 512 else 512, 128)
    n_rt = seq_len // tr
    tab = pl.BlockSpec((tr, seq_len), lambda r, s, j: (r, 0))
    src = pl.BlockSpec((seq_len, tn), lambda r, s, j: (sb0 + s, j))
    return pl.pallas_call(
        _fourier_pos_kernel,
        grid=(n_rt, n_seq, width // tn),
        in_specs=[tab, tab, src, src],
        out_specs=pl.BlockSpec((tr, tn), lambda r, s, j: (s * n_rt + r, j)),
        out_shape=jax.ShapeDtypeStruct((n_seq * seq_len, width), BF16),
        compiler_params=_cparams(("arbitrary", "arbitrary", "arbitrary"),
                                 8 * tr * seq_len + 8 * seq_len * tn + 12 * tr * tn + (8 << 20)),
        name="fourier_pos",
    )(ct, st, a_cos, a_sin)


def _rope_tables(seq_len):
    half = HEAD_DIM // 2
    t = np.arange(seq_len)
    inv_freq = (1.0 / (np.float32(ROPE_BASE) ** (np.arange(0, half, 2, dtype=np.float32) / np.float32(half)))).astype(np.float32)
    ang_row = ((t // GRID_W).astype(np.float32)[:, None] * inv_freq).astype(np.float64)
    ang_col = ((t % GRID_W).astype(np.float32)[:, None] * inv_freq).astype(np.float64)
    cos = np.concatenate([np.cos(ang_row)] * 2 + [np.cos(ang_col)] * 2, axis=1)
    sin = np.concatenate([-np.sin(ang_row), np.sin(ang_row), -np.sin(ang_col), np.sin(ang_col)], axis=1)
    return jnp.asarray(cos, dtype=F32), jnp.asarray(sin, dtype=F32)


def _gate_layouts(p_ab, heads):
    nt = p_ab.shape[0]
    per_head = p_ab[:, :4 * heads].reshape(nt // GROUP, GROUP, 4, heads)
    return per_head.transpose(3, 0, 1, 2), per_head.transpose(3, 0, 2, 1)


def kernel(x_prompt, x_sample, cache_na_k, cache_na_v, state_gdn, c, c_ctx, norm1_w, norm2_w, ada_w, ada_b, in_w, conv_w, gdn_a_log, gdn_dt_bias, gdn_norm_w, na_q_norm_w, na_k_norm_w, na_rpb, out_w, ffn_gate_w, ffn_up_w, ffn_down_w):
    batch, seq, d = x_prompt.shape
    dec_batch, dec_seq, _ = x_sample.shape
    depth = in_w.shape[0]
    gh = state_gdn.shape[3]
    nh = cache_na_k.shape[3]
    gw, nw = gh * HEAD_DIM, nh * HEAD_DIM
    fw = in_w.shape[2] - 4 * gw - 4 * gh - 3 * nw
    fg = fw // HEAD_DIM
    n_ctx, n_lat = batch * seq, dec_batch * dec_seq
    off_ab = 4 * gw
    off_na = off_ab + 4 * gh
    off_fn = off_na + 3 * nw
    assert dec_batch + 1 <= MOD_ROWS and 4 * gh <= HEAD_DIM
    win_rows = min(NA_ROWS, dec_seq // GRID_W)

    cond = jnp.concatenate([c_ctx[None, :], c, jnp.zeros((MOD_ROWS - 1 - dec_batch, d), F32)], axis=0)
    mods_all = _adaln(cond, ada_w, ada_b)
    rope_tabs = _rope_tables(dec_seq)

    x = (x_prompt.reshape(n_ctx, d), x_sample.reshape(n_lat, d))
    new_k, new_v, new_s = [], [], []
    for l in range(depth):
        mods = mods_all[l].reshape(MOD_ROWS * 6, 1, d)
        w_in = in_w[l]
        w_ab = jnp.pad(w_in[:, off_ab:off_na], ((0, 0), (0, HEAD_DIM - 4 * gh))).astype(BF16)
        w_na = w_in[:, off_na:off_fn].astype(BF16)
        w_fn = w_in[:, off_fn:].astype(BF16)

        h = _modulate(x, norm1_w[l], mods, 0, 1, n_ctx, dec_seq)
        p_gdn = _matmul(h, w_in[:, :off_ab].astype(BF16), F32, "proj_gdn")
        p_ab = _matmul(h, w_ab, F32, "proj_gates")
        p_na = _matmul(h, w_na, F32, "proj_na")
        p_fn = _matmul(h, w_fn, F32, "proj_fn")

        gate_col, gate_row = _gate_layouts(p_ab, gh)
        gdn_common = (p_gdn, gate_col, gate_row, conv_w[l], gdn_a_log[l], gdn_dt_bias[l], gdn_norm_w[l])
        gdn_ctx, s_ctx = _gdn(*gdn_common, row0=0, n_seq=batch, seq_len=seq, heads=gh, emit_state=True)
        gdn_lat = _gdn(*gdn_common, row0=n_ctx, n_seq=dec_batch, seq_len=dec_seq, heads=gh,
                       rope_tabs=rope_tabs, s0=state_gdn, layer=l)

        nk_ctx, nv_ctx, na_ctx = _ctx_attention(p_na, na_q_norm_w[l], na_k_norm_w[l], n_seq=batch, seq_len=seq, heads=nh)
        bias = _bias_table(na_rpb[l], win_rows)
        na_lat = _lat_attention(p_na, cache_na_k, cache_na_v, na_q_norm_w[l], na_k_norm_w[l], bias,
                                row0=n_ctx, n_seq=dec_batch, seq_len=dec_seq, heads=nh, layer=l)

        a_cos, a_sin = _fourier_chan(p_fn, fg)
        fn_ctx = _fourier_pos(a_cos, a_sin, row0=0, n_seq=batch, seq_len=seq)
        fn_lat = _fourier_pos(a_cos, a_sin, row0=n_ctx, n_seq=dec_batch, seq_len=dec_seq)

        x = _outproj([gdn_ctx, na_ctx, fn_ctx], [gdn_lat, na_lat, fn_lat], _cast_layer(out_w, l),
                     x, mods, 2, n_ctx, dec_seq)

        h2 = _modulate(x, norm2_w[l], mods, 3, 4, n_ctx, dec_seq)
        act = _swiglu(h2, ffn_gate_w, ffn_up_w, l)
        w_down = _cast_layer(ffn_down_w, l)
        if l < depth - 1:
            x = _down(act, w_down, x, mods, 5, n_ctx, dec_seq)
        else:
            y_ctx = _down(act, w_down, x, mods, 5, n_ctx, dec_seq, 0, n_ctx)
            y_lat = _down(act, w_down, x, mods, 5, n_ctx, dec_seq, n_ctx, n_lat)

        new_k.append(nk_ctx.reshape(batch, seq, nh, HEAD_DIM))
        new_v.append(nv_ctx.reshape(batch, seq, nh, HEAD_DIM))
        new_s.append(s_ctx)

    y_prompt = y_ctx.reshape(batch, seq, d)
    y_sample = y_lat.reshape(dec_batch, dec_seq, d)
    return (y_prompt, y_sample, jnp.stack(new_k, axis=1), jnp.stack(new_v, axis=1), jnp.stack(new_s, axis=1))
```
